```python
import math
import jax, jax.numpy as jnp
from jax import lax
import numpy as np

D_MODEL = 1024
BATCH = 32
SEQ = 256
DEPTH = 4
DEC_BATCH = 4
DEC_SEQ = 1024
PAST_LEN = 512

GRID_W = 64
N_MIXERS = 2
N_NA_LAYERS = (DEPTH + N_MIXERS - 1) // N_MIXERS
N_SSD_LAYERS = DEPTH // N_MIXERS
NA_HEADS = 16
NA_HEAD_DIM = D_MODEL // NA_HEADS
NA_WIN_H = 8
NA_WIN_W = 16
DENSE_BLOCK_KEYS = 2048
Q_BLOCK = 128
SSD_EXPAND = 2
SSD_D_INNER = SSD_EXPAND * D_MODEL
SSD_HEAD_DIM = 64
SSD_HEADS = SSD_D_INNER // SSD_HEAD_DIM
SSD_GROUPS = 8
SSD_D_STATE = 128
SSD_CONV = 3
SSD_CHUNK = 128
SSD_CONV_DIM = SSD_D_INNER + 2 * SSD_GROUPS * SSD_D_STATE
SSD_IN_DIM = SSD_D_INNER + SSD_CONV_DIM + 2 * SSD_HEADS
D_FF = -(-8 * D_MODEL // (3 * 256)) * 256
RMS_EPS = 1e-6

kernel_name = 'hybrid_na_ssd_dit_step'


def rmsnorm(x, g):
    xf = x.astype(jnp.float32)
    y = xf * lax.rsqrt(jnp.mean(xf * xf, axis=-1, keepdims=True) + RMS_EPS)
    return (y * g.astype(jnp.float32)).astype(x.dtype)


def ada_params(cond, w, b):
    m = jax.nn.silu(cond) @ w + b
    return jnp.split(m[:, None, :], 6, axis=-1)


def modulate(h, shift, scale):
    return h * (1 + scale) + shift


def swiglu(h, w_gate_up, w_down):
    g, u = jnp.split(h @ w_gate_up, 2, axis=-1)
    return (jax.nn.silu(g) * u) @ w_down


def na_qkv(h, w_qkv):
    n, l, _ = h.shape
    qkv = (h @ w_qkv).reshape(n, l, 3, NA_HEADS, NA_HEAD_DIM)
    return qkv[:, :, 0], qkv[:, :, 1], qkv[:, :, 2]


def dense_attention(q, k, v):
    scale = NA_HEAD_DIM ** -0.5

    def attend(qb):
        s = jnp.einsum('bqhd,bhkd->bhqk', qb, k).astype(jnp.float32) * scale
        p = jax.nn.softmax(s, axis=-1).astype(v.dtype)
        return jnp.einsum('bhqk,bhkd->bqhd', p, v)

    n, lq = q.shape[:2]
    if k.shape[2] < DENSE_BLOCK_KEYS:
        return attend(q)
    qb = q.reshape(n, lq // Q_BLOCK, Q_BLOCK, NA_HEADS, NA_HEAD_DIM).swapaxes(0, 1)
    out = lax.map(attend, qb)
    return out.swapaxes(0, 1).reshape(q.shape)


def neighbourhood_attention(q, k, v, k_ctx, v_ctx, rpb):
    n, t, nh, hd = q.shape
    rows = t // GRID_W
    kh = min(NA_WIN_H, rows)
    kw = NA_WIN_W
    scale = hd ** -0.5
    qg = q.reshape(n, rows, GRID_W, nh, hd)
    kg = k.reshape(n, rows, GRID_W, nh, hd)
    vg = v.reshape(n, rows, GRID_W, nh, hd)
    cols = jnp.arange(GRID_W)
    col_start = jnp.clip(cols - kw // 2, 0, GRID_W - kw)
    col_mask = (cols[None, :] >= col_start[:, None]) & (cols[None, :] < col_start[:, None] + kw)
    dc_idx = jnp.clip(cols[None, :] - cols[:, None] + kw - 1, 0, 2 * kw - 2)

    def row_block(args):
        r, q_r = args
        r0 = jnp.clip(r - kh // 2, 0, rows - kh)
        k_r = lax.dynamic_slice_in_dim(kg, r0, kh, axis=1)
        v_r = lax.dynamic_slice_in_dim(vg, r0, kh, axis=1)
        dr_idx = r0 + jnp.arange(kh) - r + NA_WIN_H - 1
        bias = rpb[:, dr_idx[None, :, None], dc_idx[:, None, :]].astype(jnp.float32)
        s_loc = jnp.einsum('bqhd,bkwhd->bhqkw', q_r, k_r).astype(jnp.float32) * scale + bias
        s_loc = jnp.where(col_mask[:, None, :], s_loc, -jnp.inf).reshape(n, nh, GRID_W, kh * GRID_W)
        s_ctx = jnp.einsum('bqhd,bhcd->bhqc', q_r, k_ctx).astype(jnp.float32) * scale
        p = jax.nn.softmax(jnp.concatenate([s_loc, s_ctx], axis=-1), axis=-1).astype(v.dtype)
        p_loc = p[..., :kh * GRID_W].reshape(n, nh, GRID_W, kh, GRID_W)
        p_ctx = p[..., kh * GRID_W:]
        return (jnp.einsum('bhqkw,bkwhd->bqhd', p_loc, v_r)
                + jnp.einsum('bhqc,bhcd->bqhd', p_ctx, v_ctx))

    out = lax.map(row_block, (jnp.arange(rows), qg.swapaxes(0, 1)))
    return out.swapaxes(0, 1).reshape(n, t, nh, hd)


def na_context_mixer(h, w_qkv, w_o):
    n, l, _ = h.shape
    q, k, v = na_qkv(h, w_qkv)
    k = k.transpose(0, 2, 1, 3)
    v = v.transpose(0, 2, 1, 3)
    o = dense_attention(q, k, v)
    return o.reshape(n, l, D_MODEL) @ w_o, k, v


def na_latent_mixer(h, k_ctx, v_ctx, w_qkv, w_o, rpb):
    n, l, _ = h.shape
    q, k, v = na_qkv(h, w_qkv)
    o = neighbourhood_attention(q, k, v, k_ctx, v_ctx, rpb)
    return o.reshape(n, l, D_MODEL) @ w_o


def segsum(a):
    cs = jnp.cumsum(a, axis=-1)
    t = a.shape[-1]
    d = cs[..., :, None] - cs[..., None, :]
    return jnp.where(jnp.tril(jnp.ones((t, t), dtype=bool)), d, -jnp.inf)


def ssd_chunked(x, da, bm, cm, init):
    n, l, nh, p = x.shape
    g, ns = bm.shape[2], bm.shape[3]
    r = nh // g
    q = SSD_CHUNK
    nc = l // q
    x = x.reshape(n, nc, q, g, r, p)
    bm = bm.reshape(n, nc, q, g, ns)
    cm = cm.reshape(n, nc, q, g, ns)
    a = da.reshape(n, nc, q, g, r).transpose(0, 3, 4, 1, 2)
    a_cs = jnp.cumsum(a, axis=-1)
    lmat = jnp.exp(segsum(a))
    cb = jnp.einsum('bclgn,bcsgn->bcgls', cm, bm)
    y_diag = jnp.einsum('bcgls,bgrcls,bcsgrp->bclgrp', cb, lmat, x)
    decay_states = jnp.exp(a_cs[..., -1:] - a_cs)
    states = jnp.einsum('bclgn,bgrcl,bclgrp->bcgrpn', bm, decay_states, x)
    states = jnp.concatenate([init.reshape(n, 1, g, r, p, ns), states], axis=1)
    chunk_a = jnp.pad(a_cs[..., -1], ((0, 0), (0, 0), (0, 0), (1, 0)))
    decay_chunk = jnp.exp(segsum(chunk_a))
    new_states = jnp.einsum('bgrzc,bcgrpn->bzgrpn', decay_chunk, states)
    states, final = new_states[:, :-1], new_states[:, -1]
    y_off = jnp.einsum('bclgn,bcgrpn,bgrcl->bclgrp', cm, states, jnp.exp(a_cs))
    y = (y_diag + y_off).reshape(n, l, nh, p)
    return y, final.reshape(n, nh, p, ns)


def depthwise_conv_centred(u, w, b):
    kw = w.shape[0]
    pad = kw // 2
    out = lax.conv_general_dilated(u, w[:, None, :].astype(u.dtype), window_strides=(1,),
                                   padding=[(pad, kw - 1 - pad)],
                                   dimension_numbers=('NWC', 'WIO', 'NWC'),
                                   feature_group_count=u.shape[-1])
    return out + b


def ssd_project(h, w_in, conv_w, conv_b):
    zxbcdt = h @ w_in
    z = zxbcdt[..., :SSD_D_INNER]
    xbc = zxbcdt[..., SSD_D_INNER:SSD_D_INNER + SSD_CONV_DIM]
    dt_raw = zxbcdt[..., SSD_D_INNER + SSD_CONV_DIM:]
    xbc = jax.nn.silu(depthwise_conv_centred(xbc, conv_w, conv_b))
    return z, xbc, dt_raw


def ssd_bidir_scan(xbc, dt_raw, dt_bias, a_log, init_f, init_b):
    n, l, _ = xbc.shape
    gn = SSD_GROUPS * SSD_D_STATE
    f32 = jnp.float32
    x = xbc[..., :SSD_D_INNER].astype(f32).reshape(n, l, SSD_HEADS, SSD_HEAD_DIM)
    bm = xbc[..., SSD_D_INNER:SSD_D_INNER + gn].astype(f32).reshape(n, l, SSD_GROUPS, SSD_D_STATE)
    cm = xbc[..., SSD_D_INNER + gn:].astype(f32).reshape(n, l, SSD_GROUPS, SSD_D_STATE)
    dt = jax.nn.softplus(dt_raw.astype(f32).reshape(n, l, 2, SSD_HEADS) + dt_bias.astype(f32))
    da = dt * (-jnp.exp(a_log.astype(f32)))
    xdt = x[:, :, None] * dt[..., None]
    y_f, s_f = ssd_chunked(xdt[:, :, 0], da[:, :, 0], bm, cm, init_f)
    flip = lambda a: jnp.flip(a, axis=1)
    y_b, s_b = ssd_chunked(flip(xdt[:, :, 1]), flip(da[:, :, 1]), flip(bm), flip(cm), init_b)
    return y_f + flip(y_b), x, s_f, s_b


def ssd_output(y, x, z, d_skip, norm_g, w_out):
    n, l = z.shape[:2]
    y = (y + x * d_skip.astype(jnp.float32)[:, None]).reshape(n, l, SSD_D_INNER)
    y = rmsnorm(y * jax.nn.silu(z.astype(jnp.float32)), norm_g)
    return y.astype(z.dtype) @ w_out


def ssd_context_mixer(h, w_in, conv_w, conv_b, dt_bias, a_log, d_skip, norm_g, w_out):
    n = h.shape[0]
    z, xbc, dt_raw = ssd_project(h, w_in, conv_w, conv_b)
    zeros = jnp.zeros((n, SSD_HEADS, SSD_HEAD_DIM, SSD_D_STATE), jnp.float32)
    y, x, s_f, s_b = ssd_bidir_scan(xbc, dt_raw, dt_bias, a_log, zeros, zeros)
    out = ssd_output(y, x, z, d_skip, norm_g, w_out)
    return out, jnp.stack([s_f, s_b], axis=1).astype(h.dtype)


def ssd_latent_mixer(h, state, w_in, conv_w, conv_b, dt_bias, a_log, d_skip, norm_g, w_out):
    z, xbc, dt_raw = ssd_project(h, w_in, conv_w, conv_b)
    init_f = state[:, 0].astype(jnp.float32)
    init_b = state[:, 1].astype(jnp.float32)
    y, x, _, _ = ssd_bidir_scan(xbc, dt_raw, dt_bias, a_log, init_f, init_b)
    return ssd_output(y, x, z, d_skip, norm_g, w_out)


def setup_inputs(seed: int = 0) -> dict:
    key = jax.random.key(seed)
    ks = jax.random.split(key, 26)
    f32 = jnp.float32
    D = D_MODEL

    def nrm(k, shape, s):
        return jax.random.normal(k, shape, f32) * s

    dt0 = jnp.exp(jax.random.uniform(ks[18], (N_SSD_LAYERS, 2, SSD_HEADS), f32,
                                     math.log(1e-3), math.log(1e-1)))
    return {
        'x_prompt': nrm(ks[0], (BATCH, SEQ, D), 1.0),
        'x_sample': nrm(ks[1], (DEC_BATCH, DEC_SEQ, D), 1.0),
        'cache_k': nrm(ks[2], (DEC_BATCH, N_NA_LAYERS, NA_HEADS, PAST_LEN, NA_HEAD_DIM), 1.0),
        'cache_v': nrm(ks[3], (DEC_BATCH, N_NA_LAYERS, NA_HEADS, PAST_LEN, NA_HEAD_DIM), 1.0),
        'state_ssm': nrm(ks[4], (DEC_BATCH, N_SSD_LAYERS, 2, SSD_HEADS, SSD_HEAD_DIM, SSD_D_STATE), 0.1),
        'c': nrm(ks[5], (DEC_BATCH, D), 1.0),
        'c_ctx': nrm(ks[6], (D,), 1.0),
        'ada_w': nrm(ks[7], (DEPTH, D, 6 * D), 0.5 * D ** -0.5),
        'ada_b': nrm(ks[8], (DEPTH, 6 * D), 0.01),
        'norm_mix_g': 1.0 + nrm(ks[9], (DEPTH, D), 0.02),
        'norm_ffn_g': 1.0 + nrm(ks[10], (DEPTH, D), 0.02),
        'ffn_w_gate_up': nrm(ks[11], (DEPTH, D, 2 * D_FF), D ** -0.5),
        'ffn_w_down': nrm(ks[12], (DEPTH, D_FF, D), D_FF ** -0.5),
        'na_w_qkv': nrm(ks[13], (N_NA_LAYERS, D, 3 * D), D ** -0.5),
        'na_w_o': nrm(ks[14], (N_NA_LAYERS, D, D), D ** -0.5),
        'na_rpb': nrm(ks[15], (N_NA_LAYERS, NA_HEADS, 2 * NA_WIN_H - 1, 2 * NA_WIN_W - 1), 0.1),
        'ssd_w_in': nrm(ks[16], (N_SSD_LAYERS, D, SSD_IN_DIM), D ** -0.5),
        'ssd_conv_w': nrm(ks[17], (N_SSD_LAYERS, SSD_CONV, SSD_CONV_DIM), SSD_CONV ** -0.5),
        'ssd_conv_b': nrm(ks[19], (N_SSD_LAYERS, SSD_CONV_DIM), 0.01),
        'ssd_dt_bias': dt0 + jnp.log(-jnp.expm1(-dt0)),
        'ssd_a_log': jnp.log(jax.random.uniform(ks[20], (N_SSD_LAYERS, 2, SSD_HEADS), f32, 1.0, 16.0)),
        'ssd_d': 1.0 + nrm(ks[21], (N_SSD_LAYERS, SSD_HEADS), 0.02),
        'ssd_norm_g': 1.0 + nrm(ks[22], (N_SSD_LAYERS, SSD_D_INNER), 0.02),
        'ssd_w_out': nrm(ks[23], (N_SSD_LAYERS, SSD_D_INNER, D), SSD_D_INNER ** -0.5),
        'final_norm_g': 1.0 + nrm(ks[24], (D,), 0.02),
    }


def reference(x_prompt, x_sample, cache_k, cache_v, state_ssm, c, c_ctx,
              ada_w, ada_b, norm_mix_g, norm_ffn_g, ffn_w_gate_up, ffn_w_down,
              na_w_qkv, na_w_o, na_rpb,
              ssd_w_in, ssd_conv_w, ssd_conv_b, ssd_dt_bias, ssd_a_log, ssd_d, ssd_norm_g, ssd_w_out,
              final_norm_g):
    xp, xs = x_prompt, x_sample
    new_k, new_v, new_s = [], [], []
    for i in range(DEPTH):
        j = i // N_MIXERS
        mp = ada_params(c_ctx[None, :], ada_w[i], ada_b[i])
        ms = ada_params(c, ada_w[i], ada_b[i])
        hp = modulate(rmsnorm(xp, norm_mix_g[i]), mp[0], mp[1])
        hs = modulate(rmsnorm(xs, norm_mix_g[i]), ms[0], ms[1])
        if i % N_MIXERS == 0:
            op, k_p, v_p = na_context_mixer(hp, na_w_qkv[j], na_w_o[j])
            os_ = na_latent_mixer(hs, cache_k[:, j], cache_v[:, j], na_w_qkv[j], na_w_o[j], na_rpb[j])
            new_k.append(k_p)
            new_v.append(v_p)
        else:
            op, s_p = ssd_context_mixer(hp, ssd_w_in[j], ssd_conv_w[j], ssd_conv_b[j], ssd_dt_bias[j],
                                        ssd_a_log[j], ssd_d[j], ssd_norm_g[j], ssd_w_out[j])
            os_ = ssd_latent_mixer(hs, state_ssm[:, j], ssd_w_in[j], ssd_conv_w[j], ssd_conv_b[j],
                                   ssd_dt_bias[j], ssd_a_log[j], ssd_d[j], ssd_norm_g[j], ssd_w_out[j])
            new_s.append(s_p)
        xp = xp + mp[2] * op
        xs = xs + ms[2] * os_
        hp = modulate(rmsnorm(xp, norm_ffn_g[i]), mp[3], mp[4])
        hs = modulate(rmsnorm(xs, norm_ffn_g[i]), ms[3], ms[4])
        xp = xp + mp[5] * swiglu(hp, ffn_w_gate_up[i], ffn_w_down[i])
        xs = xs + ms[5] * swiglu(hs, ffn_w_gate_up[i], ffn_w_down[i])
    y_prompt = rmsnorm(xp, final_norm_g)
    y_sample = rmsnorm(xs, final_norm_g)
    new_state = (jnp.stack(new_k, axis=1), jnp.stack(new_v, axis=1), jnp.stack(new_s, axis=1))
    return (y_prompt, y_sample, *new_state)
```

```python
import functools

import jax
import jax.numpy as jnp
from jax import lax
from jax.experimental import pallas as pl
from jax.experimental.pallas import tpu as pltpu

F32 = jnp.float32
BF16 = jnp.bfloat16

D = 1024
N_CTX, L_CTX = 32, 256
N_LAT, L_LAT = 4, 1024
T_CTX = N_CTX * L_CTX
T_LAT = N_LAT * L_LAT
T_ALL = T_CTX + T_LAT
DEPTH = 4
N_COND = 8
PAST = 512
GRID_W = 64
GRID_H = L_LAT // GRID_W
HEADS = 16
HEAD_DIM = 64
HEAD_PAIRS = HEADS // 2
WIN_H, WIN_W = 8, 16
RPB_H, RPB_W = 2 * WIN_H - 1, 2 * WIN_W - 1
D_INNER = 2048
SSD_HEADS = 32
SSD_P = 64
SSD_GROUPS = 8
SSD_R = SSD_HEADS // SSD_GROUPS
SSD_N = 128
CHUNK = 128
CONV_DIM = D_INNER + 2 * SSD_GROUPS * SSD_N
ZX_DIM = D_INNER + CONV_DIM
D_FF = 2816
EPS = 1e-6
NEG_INF = float("-inf")

TM = 1024
TM_FFN = 512
TF = D_FF // 2
TN_QKV = 1024
TN_IN = 1024
TM_OUT = 512
BIAS_BLOCKS = 18
VMEM_LIMIT = 56 * 1024 * 1024


def _cparams(*sem):
    return pltpu.CompilerParams(dimension_semantics=sem, vmem_limit_bytes=VMEM_LIMIT)


def _cond_of_tile(i, tm):
    return jnp.maximum((i * tm) // L_LAT - (T_CTX // L_LAT - 1), 0)


def _silu(x):
    return x / (1.0 + jnp.exp(-x))


def _rms(x, g):
    return x * lax.rsqrt(jnp.mean(x * x, axis=-1, keepdims=True) + EPS) * g


def _norm_mod(x, g, shift, scale):
    return _rms(x, g) * (1.0 + scale) + shift


def _dot(a, b):
    return jnp.dot(a, b, preferred_element_type=F32)


def _dot_nt(a, b):
    return lax.dot_general(a, b, (((1,), (1,)), ((), ())), preferred_element_type=F32)


def _split3(x):
    hi = x.astype(BF16)
    r = x - hi.astype(F32)
    mid = r.astype(BF16)
    lo = (r - mid.astype(F32)).astype(BF16)
    return hi, mid, lo


def _ada_kernel(c_ref, w_ref, b_ref, o_ref):
    s = _silu(c_ref[...]).astype(BF16)
    o_ref[...] = _dot(s, w_ref[...].astype(BF16)) + b_ref[...]


def _ada_call(cond, ada_w, ada_b):
    tn = 1536
    return pl.pallas_call(
        _ada_kernel,
        out_shape=jax.ShapeDtypeStruct((DEPTH, N_COND, 6 * D), F32),
        grid=(DEPTH, 6 * D // tn),
        in_specs=[
            pl.BlockSpec((N_COND, D), lambda l, j: (0, 0)),
            pl.BlockSpec((None, D, tn), lambda l, j: (l, 0, j)),
            pl.BlockSpec((None, 1, tn), lambda l, j: (l, 0, j)),
        ],
        out_specs=pl.BlockSpec((None, N_COND, tn), lambda l, j: (l, 0, j)),
        compiler_params=_cparams("arbitrary", "arbitrary"),
        name="ada",
    )(cond, ada_w, ada_b.reshape(DEPTH, 1, 6 * D))


def _mod_spec(tm):
    return pl.BlockSpec((None, 1, D), lambda i, *_: (_cond_of_tile(i, tm), 0, 0))


def _qkv_kernel(x_ref, g_ref, sh_ref, sc_ref, w_ref, o_ref, h_scr):
    @pl.when(pl.program_id(1) == 0)
    def _():
        h_scr[...] = _norm_mod(x_ref[...], g_ref[...], sh_ref[...], sc_ref[...]).astype(BF16)

    o_ref[...] = _dot(h_scr[...], w_ref[...])


def _qkv_call(x, g, shift, scale, w):
    n = w.shape[1]
    return pl.pallas_call(
        _qkv_kernel,
        out_shape=jax.ShapeDtypeStruct((T_ALL, n), F32),
        grid=(T_ALL // TM, n // TN_QKV),
        in_specs=[
            pl.BlockSpec((TM, D), lambda i, j: (i, 0)),
            pl.BlockSpec((1, D), lambda i, j: (0, 0)),
            _mod_spec(TM),
            _mod_spec(TM),
            pl.BlockSpec((D, TN_QKV), lambda i, j: (0, j)),
        ],
        out_specs=pl.BlockSpec((TM, TN_QKV), lambda i, j: (i, j)),
        scratch_shapes=[pltpu.VMEM((TM, D), BF16)],
        compiler_params=_cparams("arbitrary", "arbitrary"),
        name="na_qkv",
    )(x, g, shift, scale, w)


def _inproj_kernel(x_ref, g_ref, sh_ref, sc_ref, w_ref, wdt_ref, cw_ref, cb_ref, o_ref, dt_ref, h_scr):
    i = pl.program_id(0)
    j = pl.program_id(1)

    @pl.when(j == 0)
    def _():
        h = _norm_mod(x_ref[...], g_ref[...], sh_ref[...], sc_ref[...]).astype(BF16)
        h_scr[...] = h
        dt_ref[...] = _dot(h, wdt_ref[...])

    acc = _dot(h_scr[...], w_ref[...])

    @pl.when(j < D_INNER // TN_IN)
    def _():
        o_ref[...] = acc

    @pl.when(j >= D_INNER // TN_IN)
    def _():
        seq = jnp.where(i < T_CTX // TM, L_CTX, L_LAT)
        pos = lax.broadcasted_iota(jnp.int32, (TM, 1), 0) & (seq - 1)
        prev = jnp.where(pos == 0, 0.0, pltpu.roll(acc, 1, 0))
        nxt = jnp.where(pos == seq - 1, 0.0, pltpu.roll(acc, TM - 1, 0))
        cw = cw_ref[...]
        y = cw[0:1] * prev + cw[1:2] * acc + cw[2:3] * nxt + cb_ref[...]
        o_ref[...] = _silu(y)


def _inproj_call(x, g, shift, scale, w_zx, w_dt, conv_w, conv_b):
    return pl.pallas_call(
        _inproj_kernel,
        out_shape=(jax.ShapeDtypeStruct((T_ALL, ZX_DIM), F32),
                   jax.ShapeDtypeStruct((T_ALL, 128), F32)),
        grid=(T_ALL // TM, ZX_DIM // TN_IN),
        in_specs=[
            pl.BlockSpec((TM, D), lambda i, j: (i, 0)),
            pl.BlockSpec((1, D), lambda i, j: (0, 0)),
            _mod_spec(TM),
            _mod_spec(TM),
            pl.BlockSpec((D, TN_IN), lambda i, j: (0, j)),
            pl.BlockSpec((D, 128), lambda i, j: (0, 0)),
            pl.BlockSpec((3, TN_IN), lambda i, j: (0, j)),
            pl.BlockSpec((1, TN_IN), lambda i, j: (0, j)),
        ],
        out_specs=(pl.BlockSpec((TM, TN_IN), lambda i, j: (i, j)),
                   pl.BlockSpec((TM, 128), lambda i, j: (i, 0))),
        scratch_shapes=[pltpu.VMEM((TM, D), BF16)],
        compiler_params=_cparams("arbitrary", "arbitrary"),
        name="ssd_inproj",
    )(x, g, shift, scale, w_zx, w_dt, conv_w, conv_b)


def _two_stream_rows(i, tm):
    n_ctx = T_CTX // tm
    return jnp.minimum(i, n_ctx - 1), jnp.maximum(i - n_ctx, 0)


def _oproj_kernel(ac_ref, al_ref, x_ref, gt_ref, w_ref, o_ref):
    def run(a_ref):
        o_ref[...] = x_ref[...] + gt_ref[...] * _dot(a_ref[...], w_ref[...])

    is_ctx = pl.program_id(0) < T_CTX // TM_OUT
    pl.when(is_ctx)(lambda: run(ac_ref))
    pl.when(jnp.logical_not(is_ctx))(lambda: run(al_ref))


def _oproj_call(a_ctx, a_lat, x, gate, w):
    return pl.pallas_call(
        _oproj_kernel,
        out_shape=jax.ShapeDtypeStruct((T_ALL, D), F32),
        grid=(T_ALL // TM_OUT,),
        in_specs=[
            pl.BlockSpec((TM_OUT, D), lambda i: (_two_stream_rows(i, TM_OUT)[0], 0)),
            pl.BlockSpec((TM_OUT, D), lambda i: (_two_stream_rows(i, TM_OUT)[1], 0)),
            pl.BlockSpec((TM_OUT, D), lambda i: (i, 0)),
            _mod_spec(TM_OUT),
            pl.BlockSpec((D, D), lambda i: (0, 0)),
        ],
        out_specs=pl.BlockSpec((TM_OUT, D), lambda i: (i, 0)),
        compiler_params=_cparams("arbitrary"),
        name="na_oproj",
    )(a_ctx, a_lat, x, gate, w)


def _ssd_out_kernel(yc_ref, yl_ref, z_ref, x_ref, gt_ref, ng_ref, w_ref, o_ref):
    def run(y_ref):
        v = y_ref[...] * _silu(z_ref[...])
        v = _rms(v, ng_ref[...]).astype(BF16)
        o_ref[...] = x_ref[...] + gt_ref[...] * _dot(v, w_ref[...])

    is_ctx = pl.program_id(0) < T_CTX // TM_OUT
    pl.when(is_ctx)(lambda: run(yc_ref))
    pl.when(jnp.logical_not(is_ctx))(lambda: run(yl_ref))


def _ssd_out_call(y_ctx, y_lat, zx, x, gate, norm_g, w):
    return pl.pallas_call(
        _ssd_out_kernel,
        out_shape=jax.ShapeDtypeStruct((T_ALL, D), F32),
        grid=(T_ALL // TM_OUT,),
        in_specs=[
            pl.BlockSpec((TM_OUT, D_INNER), lambda i: (_two_stream_rows(i, TM_OUT)[0], 0)),
            pl.BlockSpec((TM_OUT, D_INNER), lambda i: (_two_stream_rows(i, TM_OUT)[1], 0)),
            pl.BlockSpec((TM_OUT, D_INNER), lambda i: (i, 0)),
            pl.BlockSpec((TM_OUT, D), lambda i: (i, 0)),
            _mod_spec(TM_OUT),
            pl.BlockSpec((1, D_INNER), lambda i: (0, 0)),
            pl.BlockSpec((D_INNER, D), lambda i: (0, 0)),
        ],
        out_specs=pl.BlockSpec((TM_OUT, D), lambda i: (i, 0)),
        compiler_params=_cparams("arbitrary"),
        name="ssd_out",
    )(y_ctx, y_lat, zx, x, gate, norm_g, w)


def _ffn_kernel(x_ref, g_ref, sh_ref, sc_ref, gt_ref, wg_ref, wu_ref, wd_ref, fg_ref, o_ref,
                h_scr, acc_scr, *, final_norm):
    j = pl.program_id(1)

    @pl.when(j == 0)
    def _():
        h_scr[...] = _norm_mod(x_ref[...], g_ref[...], sh_ref[...], sc_ref[...]).astype(BF16)

    h = h_scr[...]
    act = (_silu(_dot(h, wg_ref[...])) * _dot(h, wu_ref[...])).astype(BF16)
    part = _dot(act, wd_ref[...])

    @pl.when(j == 0)
    def _():
        acc_scr[...] = part

    @pl.when(j == D_FF // TF - 1)
    def _():
        y = x_ref[...] + gt_ref[...] * (acc_scr[...] + part)
        if final_norm:
            y = _rms(y, fg_ref[...])
        o_ref[...] = y


def _ffn_call(x, g, shift, scale, gate, w_gu, w_d, final_g, final_norm):
    nj = D_FF // TF
    assert nj == 2
    return pl.pallas_call(
        functools.partial(_ffn_kernel, final_norm=final_norm),
        out_shape=jax.ShapeDtypeStruct((T_ALL, D), F32),
        grid=(T_ALL // TM_FFN, nj),
        in_specs=[
            pl.BlockSpec((TM_FFN, D), lambda i, j: (i, 0)),
            pl.BlockSpec((1, D), lambda i, j: (0, 0)),
            _mod_spec(TM_FFN),
            _mod_spec(TM_FFN),
            _mod_spec(TM_FFN),
            pl.BlockSpec((D, TF), lambda i, j: (0, j)),
            pl.BlockSpec((D, TF), lambda i, j: (0, nj + j)),
            pl.BlockSpec((TF, D), lambda i, j: (j, 0)),
            pl.BlockSpec((1, D), lambda i, j: (0, 0)),
        ],
        out_specs=pl.BlockSpec((TM_FFN, D), lambda i, j: (i, 0)),
        scratch_shapes=[pltpu.VMEM((TM_FFN, D), BF16), pltpu.VMEM((TM_FFN, D), F32)],
        compiler_params=_cparams("arbitrary", "arbitrary"),
        name="ffn",
    )(x, g, shift, scale, gate, w_gu, w_gu, w_d, final_g)


def _bias_kernel(rpb_ref, te_ref, to_ref):
    h = pl.program_id(0)
    qc = lax.broadcasted_iota(jnp.int32, (GRID_W, 128), 0)
    lane = lax.broadcasted_iota(jnp.int32, (GRID_W, 128), 1)
    kc = lane & (GRID_W - 1)
    upper = lane >= GRID_W
    idx = jnp.clip(kc - qc + (WIN_W - 1), 0, RPB_W - 1)
    start = jnp.clip(qc - WIN_W // 2, 0, GRID_W - WIN_W)
    in_win = (kc >= start) & (kc < start + WIN_W)

    def pair_tile(dr_lo, dr_hi):
        def ok(dr):
            return 0 <= dr < RPB_H

        acc = jnp.zeros((GRID_W, 128), F32)
        for d in range(RPB_W):
            lo = rpb_ref[h * (RPB_H * RPB_W) + dr_lo * RPB_W + d] if ok(dr_lo) else 0.0
            hi = rpb_ref[h * (RPB_H * RPB_W) + dr_hi * RPB_W + d] if ok(dr_hi) else 0.0
            acc = jnp.where(idx == d, jnp.where(upper, hi, lo), acc)
        if ok(dr_lo) and ok(dr_hi):
            valid = in_win
        elif ok(dr_lo):
            valid = in_win & jnp.logical_not(upper)
        elif ok(dr_hi):
            valid = in_win & upper
        else:
            return jnp.full((GRID_W, 128), NEG_INF, F32)
        return jnp.where(valid, acc, NEG_INF)

    for t in range(BIAS_BLOCKS // 2):
        te_ref[:, t * 128:(t + 1) * 128] = pair_tile(2 * t - 1, 2 * t)
        to_ref[:, t * 128:(t + 1) * 128] = pair_tile(2 * t - 2, 2 * t - 1)


def _bias_call(rpb):
    shape = jax.ShapeDtypeStruct((HEADS, GRID_W, BIAS_BLOCKS * GRID_W), F32)
    spec = pl.BlockSpec((None, GRID_W, BIAS_BLOCKS * GRID_W), lambda h: (h, 0, 0))
    return pl.pallas_call(
        _bias_kernel,
        out_shape=(shape, shape),
        grid=(HEADS,),
        in_specs=[pl.BlockSpec(memory_space=pltpu.SMEM)],
        out_specs=(spec, spec),
        compiler_params=_cparams("arbitrary"),
        name="na_bias",
    )(rpb.reshape(-1))


def _half_mask(e):
    lane = lax.broadcasted_iota(jnp.int32, (1, 128), 1)
    return (lane >= HEAD_DIM) if e else (lane < HEAD_DIM)


def _na_ctx_kernel(qkv_ref, o_ref, ko_ref, vo_ref):
    scale = HEAD_DIM ** -0.5
    for hp in range(HEAD_PAIRS):
        q2 = qkv_ref[:, hp * 128:(hp + 1) * 128] * scale
        k2 = qkv_ref[:, D + hp * 128:D + (hp + 1) * 128]
        v2 = qkv_ref[:, 2 * D + hp * 128:2 * D + (hp + 1) * 128]
        for e in range(2):
            ko_ref[2 * hp + e] = k2[:, e * HEAD_DIM:(e + 1) * HEAD_DIM]
            vo_ref[2 * hp + e] = v2[:, e * HEAD_DIM:(e + 1) * HEAD_DIM]
        kb = k2.astype(BF16)
        vb = v2.astype(BF16)
        outs = []
        for e in range(2):
            qm = jnp.where(_half_mask(e), q2, 0.0).astype(BF16)
            s = _dot_nt(qm, kb)
            p = jnp.exp(s - jnp.max(s, axis=-1, keepdims=True))
            l = jnp.sum(p, axis=-1, keepdims=True)
            outs.append(_dot(p.astype(BF16), vb) / l)
        o_ref[:, hp * 128:(hp + 1) * 128] = jnp.where(_half_mask(0), outs[0], outs[1]).astype(BF16)


def _na_ctx_call(qkv):
    kv_shape = jax.ShapeDtypeStruct((N_CTX, HEADS, L_CTX, HEAD_DIM), F32)
    kv_spec = pl.BlockSpec((None, HEADS, L_CTX, HEAD_DIM), lambda n: (n, 0, 0, 0))
    return pl.pallas_call(
        _na_ctx_kernel,
        out_shape=(jax.ShapeDtypeStruct((T_CTX, D), BF16), kv_shape, kv_shape),
        grid=(N_CTX,),
        in_specs=[pl.BlockSpec((L_CTX, 3 * D), lambda n: (n, 0))],
        out_specs=(pl.BlockSpec((L_CTX, D), lambda n: (n, 0)), kv_spec, kv_spec),
        compiler_params=_cparams("arbitrary"),
        name="na_ctx",
    )(qkv)


def _window_of_row(r):
    r0 = min(max(r - WIN_H // 2, 0), GRID_H - WIN_H)
    r0a = r0 & ~1
    rows = WIN_H if r0a == r0 else WIN_H + 2
    return r0, r0a, rows


def _na_lat_kernel(q_ref, k_ref, v_ref, kc_ref, vc_ref, te_ref, to_ref, o_ref,
                   s_scr, sc_scr, p_scr, pc_scr, li_scr):
    @pl.when((pl.program_id(0) == 0) & (pl.program_id(1) == 0))
    def _():
        p_scr[...] = jnp.zeros_like(p_scr)

    q2 = q_ref[...] * (HEAD_DIM ** -0.5)
    kb = k_ref[...].astype(BF16)
    vb = v_ref[...].astype(BF16)
    kcb = kc_ref[...].astype(BF16)
    vcb = vc_ref[...].astype(BF16)
    outs = []
    for e in range(2):
        qm = jnp.where(_half_mask(e), q2, 0.0).astype(BF16)
        s_scr[...] = _dot_nt(qm, kb)
        sc_scr[...] = _dot_nt(qm, kcb)
        for r in range(GRID_H):
            r0, r0a, nrows = _window_of_row(r)
            width = nrows * GRID_W
            blk = r0a - r + WIN_H
            t_ref, t_off = (te_ref, blk * GRID_W) if r % 2 == 0 else (to_ref, (blk + 1) * GRID_W)
            rows = slice(r * GRID_W, (r + 1) * GRID_W)
            cols = slice(r0a * GRID_W, r0a * GRID_W + width)
            sl = s_scr[rows, cols] + t_ref[e, :, t_off:t_off + width]
            if nrows != WIN_H:
                lane = lax.broadcasted_iota(jnp.int32, (1, width), 1)
                sl = jnp.where((lane < GRID_W) | (lane >= width - GRID_W), NEG_INF, sl)
            sc = sc_scr[rows, :]
            m = jnp.maximum(jnp.max(sl, axis=-1, keepdims=True), jnp.max(sc, axis=-1, keepdims=True))
            p_loc = jnp.exp(sl - m)
            p_ctx = jnp.exp(sc - m)
            l = jnp.sum(p_loc, axis=-1, keepdims=True) + jnp.sum(p_ctx, axis=-1, keepdims=True)
            p_scr[rows, cols] = p_loc.astype(BF16)
            pc_scr[rows, :] = p_ctx.astype(BF16)
            li_scr[rows, :] = jnp.broadcast_to(1.0 / l, (GRID_W, 128))
        outs.append((_dot(p_scr[...], vb) + _dot(pc_scr[...], vcb)) * li_scr[...])
    o_ref[...] = jnp.where(_half_mask(0), outs[0], outs[1]).astype(BF16)


def _na_lat_call(qkv, kc_pairs, vc_pairs, t_even, t_odd):
    row0 = T_CTX // L_LAT
    ctx_spec = pl.BlockSpec((None, None, PAST, 128), lambda b, hp: (b, hp, 0, 0))
    t_spec = pl.BlockSpec((2, GRID_W, BIAS_BLOCKS * GRID_W), lambda b, hp: (hp, 0, 0))
    return pl.pallas_call(
        _na_lat_kernel,
        out_shape=jax.ShapeDtypeStruct((T_LAT, D), BF16),
        grid=(N_LAT, HEAD_PAIRS),
        in_specs=[
            pl.BlockSpec((L_LAT, 128), lambda b, hp: (row0 + b, hp)),
            pl.BlockSpec((L_LAT, 128), lambda b, hp: (row0 + b, HEAD_PAIRS + hp)),
            pl.BlockSpec((L_LAT, 128), lambda b, hp: (row0 + b, 2 * HEAD_PAIRS + hp)),
            ctx_spec, ctx_spec, t_spec, t_spec,
        ],
        out_specs=pl.BlockSpec((L_LAT, 128), lambda b, hp: (b, hp)),
        scratch_shapes=[
            pltpu.VMEM((L_LAT, L_LAT), F32),
            pltpu.VMEM((L_LAT, PAST), F32),
            pltpu.VMEM((L_LAT, L_LAT), BF16),
            pltpu.VMEM((L_LAT, PAST), BF16),
            pltpu.VMEM((L_LAT, 128), F32),
        ],
        compiler_params=_cparams("arbitrary", "arbitrary"),
        name="na_lat",
    )(qkv, qkv, qkv, kc_pairs, vc_pairs, t_even, t_odd)


def _pair_heads(cache):
    b, h, l, hd = cache.shape
    return cache.reshape(b, h // 2, 2, l, hd).transpose(0, 1, 3, 2, 4).reshape(b, h // 2, l, 2 * hd)


def _ssd_kernel(*refs, seq, has_init, has_final):
    it = iter(refs)
    x_ref, b_ref, c_ref, dt_ref, bias_ref, nega_ref, d_ref = (next(it) for _ in range(7))
    init_ref = next(it) if has_init else None
    y_ref = next(it)
    fin_ref = next(it) if has_final else None
    sloc_scr, sin_scr = next(it), next(it)

    nc = seq // CHUNK
    g = pl.program_id(1)
    rows = SSD_R * SSD_P

    src = lax.broadcasted_iota(jnp.int32, (128, 128), 0)
    dst = lax.broadcasted_iota(jnp.int32, (128, 128), 1)
    want = jnp.where(dst < SSD_R, g * SSD_R + dst, SSD_HEADS + g * SSD_R + dst - SSD_R)
    sel = jnp.where((dst < 2 * SSD_R) & (src == want), 1.0, 0.0).astype(BF16)
    dt_raw = sum(_dot(p, sel) for p in _split3(dt_ref[...]))
    u = dt_raw + bias_ref[...]
    dt = jnp.maximum(u, 0.0) + jnp.log1p(jnp.exp(-jnp.abs(u)))
    a = dt * nega_ref[...]

    li = lax.broadcasted_iota(jnp.int32, (CHUNK, CHUNK), 0)
    si = lax.broadcasted_iota(jnp.int32, (CHUNK, CHUNK), 1)
    tril = jnp.where(si <= li, 1.0, 0.0).astype(BF16)
    triu = jnp.where(si >= li, 1.0, 0.0).astype(BF16)
    pre, suf = [], []
    for c in range(nc):
        parts = _split3(a[c * CHUNK:(c + 1) * CHUNK])
        pre.append(sum(_dot(tril, p) for p in parts))
        suf.append(sum(_dot(triu, p) for p in parts))
    pre = jnp.concatenate(pre, axis=0)
    suf = jnp.concatenate(suf, axis=0)
    fwd_lane = lax.broadcasted_iota(jnp.int32, (1, 128), 1) < SSD_R
    e1 = jnp.where(fwd_lane, pre, suf)
    e2 = jnp.where(fwd_lane, suf, pre) - a
    e1_t = e1.T
    dt_t = dt.T
    ysc_t = jnp.exp(e1).T
    w2_t = jnp.exp(e2).T

    def expand(t, off, cols):
        return jnp.concatenate(
            [jnp.broadcast_to(t[off + h:off + h + 1, cols], (SSD_P, cols.stop - cols.start))
             for h in range(SSD_R)], axis=0)

    x_t = x_ref[...].T

    for c in range(nc):
        cols = slice(c * CHUNK, (c + 1) * CHUNK)
        xc = x_t[:, cols]
        xw = jnp.concatenate([xc * (expand(dt_t, 0, cols) * expand(w2_t, 0, cols)),
                              xc * (expand(dt_t, SSD_R, cols) * expand(w2_t, SSD_R, cols))], axis=0)
        sloc_scr[c] = _dot(xw.astype(BF16), b_ref[cols, :].astype(BF16))

    def chunk_decay(off, col):
        return expand(ysc_t, off, slice(col, col + 1))

    s_f = init_ref[0].reshape(rows, SSD_N) if has_init else jnp.zeros((rows, SSD_N), F32)
    for c in range(nc):
        sin_scr[c, 0:rows] = s_f
        s_f = s_f * chunk_decay(0, c * CHUNK + CHUNK - 1) + sloc_scr[c, 0:rows]
    s_b = init_ref[1].reshape(rows, SSD_N) if has_init else jnp.zeros((rows, SSD_N), F32)
    for c in reversed(range(nc)):
        sin_scr[c, rows:2 * rows] = s_b
        s_b = s_b * chunk_decay(SSD_R, c * CHUNK) + sloc_scr[c, rows:2 * rows]
    if has_final:
        fin_ref[0] = s_f.reshape(SSD_R, SSD_P, SSD_N)
        fin_ref[1] = s_b.reshape(SSD_R, SSD_P, SSD_N)

    for c in range(nc):
        cols = slice(c * CHUNK, (c + 1) * CHUNK)
        bc = b_ref[cols, :].astype(BF16)
        cc = c_ref[cols, :].astype(BF16)
        cb_t = _dot_nt(bc, cc)
        xc = x_t[:, cols]
        xdf = xc * expand(dt_t, 0, cols)
        xdb = xc * expand(dt_t, SSD_R, cols)
        off = _dot_nt(sin_scr[c].astype(BF16), cc)
        off = off[0:rows] * expand(ysc_t, 0, cols) + off[rows:2 * rows] * expand(ysc_t, SSD_R, cols)
        y_heads = []
        for h in range(SSD_R):
            pf_col = e1[cols, h:h + 1]
            rb_col = e1[cols, SSD_R + h:SSD_R + h + 1]
            pf_row = e1_t[h:h + 1, cols]
            rb_row = e1_t[SSD_R + h:SSD_R + h + 1, cols]
            m_f = jnp.exp(jnp.where(li <= si, pf_row - pf_col, NEG_INF)) * cb_t
            m_b = jnp.exp(jnp.where(li >= si, rb_row - rb_col, NEG_INF)) * cb_t
            hs = slice(h * SSD_P, (h + 1) * SSD_P)
            lhs = jnp.concatenate([xdf[hs], xdb[hs]], axis=1).astype(BF16)
            rhs = jnp.concatenate([m_f, m_b], axis=0).astype(BF16)
            y_heads.append(_dot(lhs, rhs) + off[hs])
        y_t = jnp.concatenate(y_heads, axis=0)
        y_ref[cols, :] = y_t.T + x_ref[cols, :] * d_ref[...]


def _ssd_call(zx, dt, bias, nega, d_skip, state, layer, *, latent):
    seq = L_LAT if latent else L_CTX
    n_seq = N_LAT if latent else N_CTX
    row0 = T_CTX // seq if latent else 0
    nc = seq // CHUNK
    xcol0 = D_INNER // (SSD_R * SSD_P)
    bcol0 = (2 * D_INNER) // SSD_N
    ccol0 = bcol0 + SSD_GROUPS
    grp_spec = lambda w: pl.BlockSpec((None, 1, w), lambda n, g: (g, 0, 0))
    in_specs = [
        pl.BlockSpec((seq, SSD_R * SSD_P), lambda n, g: (row0 + n, xcol0 + g)),
        pl.BlockSpec((seq, SSD_N), lambda n, g: (row0 + n, bcol0 + g)),
        pl.BlockSpec((seq, SSD_N), lambda n, g: (row0 + n, ccol0 + g)),
        pl.BlockSpec((seq, 128), lambda n, g: (row0 + n, 0)),
        grp_spec(128), grp_spec(128), grp_spec(SSD_R * SSD_P),
    ]
    args = [zx, zx, zx, dt, bias, nega, d_skip]
    state_block = (None, 2, SSD_R, SSD_P, SSD_N)
    if latent:
        in_specs.append(pl.BlockSpec((None,) + state_block, lambda n, g: (n, layer, 0, g, 0, 0)))
        args.append(state)
    y_shape = jax.ShapeDtypeStruct((n_seq * seq, D_INNER), F32)
    y_spec = pl.BlockSpec((seq, SSD_R * SSD_P), lambda n, g: (n, g))
    if latent:
        out_shape, out_specs = y_shape, y_spec
    else:
        out_shape = (y_shape, jax.ShapeDtypeStruct((N_CTX, 2, SSD_HEADS, SSD_P, SSD_N), F32))
        out_specs = (y_spec, pl.BlockSpec(state_block, lambda n, g: (n, 0, g, 0, 0)))
    return pl.pallas_call(
        functools.partial(_ssd_kernel, seq=seq, has_init=latent, has_final=not latent),
        out_shape=out_shape,
        grid=(n_seq, SSD_GROUPS),
        in_specs=in_specs,
        out_specs=out_specs,
        scratch_shapes=[pltpu.VMEM((nc, 2 * SSD_R * SSD_P, SSD_N), F32),
                        pltpu.VMEM((nc, 2 * SSD_R * SSD_P, SSD_N), F32)],
        compiler_params=_cparams("arbitrary", "arbitrary"),
        name="ssd_scan_lat" if latent else "ssd_scan_ctx",
    )(*args)


def _group_lanes(p):
    q = p.reshape(2, SSD_GROUPS, SSD_R).transpose(1, 0, 2).reshape(SSD_GROUPS, 2 * SSD_R)
    return jnp.pad(q, ((0, 0), (0, 128 - 2 * SSD_R))).reshape(SSD_GROUPS, 1, 128)


def kernel(x_prompt, x_sample, cache_k, cache_v, state_ssm, c, c_ctx, ada_w, ada_b, norm_mix_g, norm_ffn_g,
           ffn_w_gate_up, ffn_w_down, na_w_qkv, na_w_o, na_rpb, ssd_w_in, ssd_conv_w, ssd_conv_b,
           ssd_dt_bias, ssd_a_log, ssd_d, ssd_norm_g, ssd_w_out, final_norm_g):
    cond = jnp.zeros((N_COND, D), F32).at[0].set(c_ctx).at[1:1 + N_LAT].set(c)
    mods = _ada_call(cond, ada_w, ada_b).reshape(DEPTH, N_COND, 6, 1, D)
    x = jnp.concatenate([x_prompt.reshape(T_CTX, D), x_sample.reshape(T_LAT, D)], axis=0)
    final_g = final_norm_g.reshape(1, D)

    new_k, new_v, new_s = [], [], []
    for i in range(DEPTH):
        j = i // 2
        m = [mods[i, :, t] for t in range(6)]
        g_mix = norm_mix_g[i].reshape(1, D)
        if i % 2 == 0:
            qkv = _qkv_call(x, g_mix, m[0], m[1], na_w_qkv[j].astype(BF16))
            o_ctx, k_new, v_new = _na_ctx_call(qkv)
            t_even, t_odd = _bias_call(na_rpb[j])
            o_lat = _na_lat_call(qkv, _pair_heads(cache_k[:, j]), _pair_heads(cache_v[:, j]), t_even, t_odd)
            x = _oproj_call(o_ctx, o_lat, x, m[2], na_w_o[j].astype(BF16))
            new_k.append(k_new)
            new_v.append(v_new)
        else:
            w_in = ssd_w_in[j].astype(BF16)
            w_dt = jnp.pad(w_in[:, ZX_DIM:], ((0, 0), (0, 128 - 2 * SSD_HEADS)))
            conv_w = jnp.pad(ssd_conv_w[j], ((0, 0), (D_INNER, 0)))
            conv_b = jnp.pad(ssd_conv_b[j], (D_INNER, 0)).reshape(1, ZX_DIM)
            zx, dt = _inproj_call(x, g_mix, m[0], m[1], w_in[:, :ZX_DIM], w_dt, conv_w, conv_b)
            bias = _group_lanes(ssd_dt_bias[j])
            nega = _group_lanes(-jnp.exp(ssd_a_log[j]))
            d_skip = jnp.repeat(ssd_d[j], SSD_P).reshape(SSD_GROUPS, 1, SSD_R * SSD_P)
            y_ctx, s_new = _ssd_call(zx, dt, bias, nega, d_skip, None, j, latent=False)
            y_lat = _ssd_call(zx, dt, bias, nega, d_skip, state_ssm, j, latent=True)
            x = _ssd_out_call(y_ctx, y_lat, zx, x, m[2], ssd_norm_g[j].reshape(1, D_INNER),
                              ssd_w_out[j].astype(BF16))
            new_s.append(s_new)
        x = _ffn_call(x, norm_ffn_g[i].reshape(1, D), m[3], m[4], m[5],
                      ffn_w_gate_up[i].astype(BF16), ffn_w_down[i].astype(BF16),
                      final_g, final_norm=(i == DEPTH - 1))

    y_prompt = x[:T_CTX].reshape(N_CTX, L_CTX, D)
    y_sample = x[T_CTX:].reshape(N_LAT, L_LAT, D)
    return (y_prompt, y_sample, jnp.stack(new_k, axis=1), jnp.stack(new_v, axis=1), jnp.stack(new_s, axis=1))
```

```python
import functools

import jax
import jax.numpy as jnp
from jax import lax
from jax.experimental import pallas as pl
from jax.experimental.pallas import tpu as pltpu

F32 = jnp.float32
BF16 = jnp.bfloat16

D = 1024
N_CTX, L_CTX = 32, 256
N_LAT, L_LAT = 4, 1024
T_CTX = N_CTX * L_CTX
T_LAT = N_LAT * L_LAT
T_ALL = T_CTX + T_LAT
DEPTH = 4
N_COND = 8
PAST = 512
GRID_W = 64
GRID_H = L_LAT // GRID_W
HEADS = 16
HEAD_DIM = 64
HEAD_PAIRS = HEADS // 2
WIN_H, WIN_W = 8, 16
RPB_H, RPB_W = 2 * WIN_H - 1, 2 * WIN_W - 1
D_INNER = 2048
SSD_HEADS = 32
SSD_P = 64
SSD_GROUPS = 8
SSD_R = SSD_HEADS // SSD_GROUPS
SSD_N = 128
CHUNK = 128
CONV_DIM = D_INNER + 2 * SSD_GROUPS * SSD_N
ZX_DIM = D_INNER + CONV_DIM
D_FF = 2816
EPS = 1e-6
NEG_INF = float("-inf")
LOG2E = 1.4426950408889634

TM = 1024
TM_FFN = 1024
TF = D_FF // 2
TN_QKV = 1024
TN_IN = 1024
TM_OUT = 512
GS_CTX = 8
GS_LAT = 2
BIAS_BLOCKS = 18
VMEM_LIMIT = 56 * 1024 * 1024


def _cparams(*sem):
    return pltpu.CompilerParams(dimension_semantics=sem, vmem_limit_bytes=VMEM_LIMIT)


def _cond_of_tile(i, tm):
    return jnp.maximum((i * tm) // L_LAT - (T_CTX // L_LAT - 1), 0)


def _silu(x):
    return x * (0.5 * jnp.tanh(0.5 * x) + 0.5)


def _rms(x, g):
    return x * lax.rsqrt(jnp.mean(x * x, axis=-1, keepdims=True) + EPS) * g


def _norm_mod(x, g, shift, scale):
    return _rms(x, g) * (1.0 + scale) + shift


def _dot(a, b):
    return jnp.dot(a, b, preferred_element_type=F32)


def _dot_nt(a, b):
    return lax.dot_general(a, b, (((1,), (1,)), ((), ())), preferred_element_type=F32)


def _split3(x):
    hi = x.astype(BF16)
    r = x - hi.astype(F32)
    mid = r.astype(BF16)
    lo = (r - mid.astype(F32)).astype(BF16)
    return hi, mid, lo


def _ada_kernel(c_ref, w_ref, b_ref, o_ref):
    s = _silu(c_ref[...]).astype(BF16)
    o_ref[...] = _dot(s, w_ref[...].astype(BF16)) + b_ref[...]


def _ada_call(cond, ada_w, ada_b):
    tn = 1536
    return pl.pallas_call(
        _ada_kernel,
        out_shape=jax.ShapeDtypeStruct((DEPTH, N_COND, 6 * D), F32),
        grid=(DEPTH, 6 * D // tn),
        in_specs=[
            pl.BlockSpec((N_COND, D), lambda l, j: (0, 0)),
            pl.BlockSpec((None, D, tn), lambda l, j: (l, 0, j)),
            pl.BlockSpec((None, 1, tn), lambda l, j: (l, 0, j)),
        ],
        out_specs=pl.BlockSpec((None, N_COND, tn), lambda l, j: (l, 0, j)),
        compiler_params=_cparams("arbitrary", "arbitrary"),
        name="ada",
    )(cond, ada_w, ada_b.reshape(DEPTH, 1, 6 * D))


def _mod_spec(tm, tile0=0):
    return pl.BlockSpec((None, 1, D), lambda i, *_: (_cond_of_tile(i + tile0, tm), 0, 0))


def _qkv_kernel(x_ref, g_ref, sh_ref, sc_ref, w_ref, o_ref, h_scr):
    @pl.when(pl.program_id(1) == 0)
    def _():
        h_scr[...] = _norm_mod(x_ref[...], g_ref[...], sh_ref[...], sc_ref[...]).astype(BF16)

    o_ref[...] = _dot(h_scr[...], w_ref[...])


def _qkv_call(x, g, shift, scale, w):
    n = w.shape[1]
    return pl.pallas_call(
        _qkv_kernel,
        out_shape=jax.ShapeDtypeStruct((T_ALL, n), F32),
        grid=(T_ALL // TM, n // TN_QKV),
        in_specs=[
            pl.BlockSpec((TM, D), lambda i, j: (i, 0)),
            pl.BlockSpec((1, D), lambda i, j: (0, 0)),
            _mod_spec(TM),
            _mod_spec(TM),
            pl.BlockSpec((D, TN_QKV), lambda i, j: (0, j)),
        ],
        out_specs=pl.BlockSpec((TM, TN_QKV), lambda i, j: (i, j)),
        scratch_shapes=[pltpu.VMEM((TM, D), BF16)],
        compiler_params=_cparams("arbitrary", "arbitrary"),
        name="na_qkv",
    )(x, g, shift, scale, w)


def _inproj_kernel(x_ref, g_ref, sh_ref, sc_ref, w_ref, wdt_ref, cw_ref, cb_ref, o_ref, dt_ref, h_scr, c_scr):
    i = pl.program_id(0)
    j = pl.program_id(1)
    half_w = TN_IN // 2

    @pl.when((i == 0) & (j == 0))
    def _():
        c_scr[:, 0:8, :] = jnp.zeros((2, 8, half_w), F32)
        c_scr[:, TM + 8:TM + 16, :] = jnp.zeros((2, 8, half_w), F32)

    @pl.when(j == 0)
    def _():
        h = _norm_mod(x_ref[...], g_ref[...], sh_ref[...], sc_ref[...]).astype(BF16)
        h_scr[...] = h
        dt_ref[...] = _dot(h, wdt_ref[...])

    @pl.when(j < D_INNER // TN_IN)
    def _():
        o_ref[...] = _dot(h_scr[...], w_ref[...])

    @pl.when(j >= D_INNER // TN_IN)
    def _():
        seq = jnp.where(i < T_CTX // TM, L_CTX, L_LAT)
        pos = lax.broadcasted_iota(jnp.int32, (TM, 1), 0) & (seq - 1)
        first = pos == 0
        last = pos == seq - 1
        for half in range(2):
            cs = slice(half * half_w, (half + 1) * half_w)
            acc = _dot(h_scr[...], w_ref[:, cs])
            c_scr[half, 8:TM + 8, :] = acc
            prev = jnp.where(first, 0.0, c_scr[half, 7:TM + 7, :])
            nxt = jnp.where(last, 0.0, c_scr[half, 9:TM + 9, :])
            y = cw_ref[0:1, cs] * prev + cw_ref[1:2, cs] * acc + cw_ref[2:3, cs] * nxt + cb_ref[:, cs]
            o_ref[:, cs] = _silu(y)


def _inproj_call(x, g, shift, scale, w_zx, w_dt, conv_w, conv_b):
    return pl.pallas_call(
        _inproj_kernel,
        out_shape=(jax.ShapeDtypeStruct((T_ALL, ZX_DIM), F32),
                   jax.ShapeDtypeStruct((T_ALL, 128), F32)),
        grid=(T_ALL // TM, ZX_DIM // TN_IN),
        in_specs=[
            pl.BlockSpec((TM, D), lambda i, j: (i, 0)),
            pl.BlockSpec((1, D), lambda i, j: (0, 0)),
            _mod_spec(TM),
            _mod_spec(TM),
            pl.BlockSpec((D, TN_IN), lambda i, j: (0, j)),
            pl.BlockSpec((D, 128), lambda i, j: (0, 0)),
            pl.BlockSpec((3, TN_IN), lambda i, j: (0, j)),
            pl.BlockSpec((1, TN_IN), lambda i, j: (0, j)),
        ],
        out_specs=(pl.BlockSpec((TM, TN_IN), lambda i, j: (i, j)),
                   pl.BlockSpec((TM, 128), lambda i, j: (i, 0))),
        scratch_shapes=[pltpu.VMEM((TM, D), BF16), pltpu.VMEM((2, TM + 16, TN_IN // 2), F32)],
        compiler_params=_cparams("arbitrary", "arbitrary"),
        name="ssd_inproj",
    )(x, g, shift, scale, w_zx, w_dt, conv_w, conv_b)


def _two_stream_rows(i, tm):
    n_ctx = T_CTX // tm
    return jnp.minimum(i, n_ctx - 1), jnp.maximum(i - n_ctx, 0)


def _oproj_kernel(ac_ref, al_ref, x_ref, gt_ref, w_ref, o_ref):
    def run(a_ref):
        o_ref[...] = x_ref[...] + gt_ref[...] * _dot(a_ref[...], w_ref[...])

    is_ctx = pl.program_id(0) < T_CTX // TM_OUT
    pl.when(is_ctx)(lambda: run(ac_ref))
    pl.when(jnp.logical_not(is_ctx))(lambda: run(al_ref))


def _oproj_call(a_ctx, a_lat, x, gate, w):
    return pl.pallas_call(
        _oproj_kernel,
        out_shape=jax.ShapeDtypeStruct((T_ALL, D), F32),
        grid=(T_ALL // TM_OUT,),
        in_specs=[
            pl.BlockSpec((TM_OUT, D), lambda i: (_two_stream_rows(i, TM_OUT)[0], 0)),
            pl.BlockSpec((TM_OUT, D), lambda i: (_two_stream_rows(i, TM_OUT)[1], 0)),
            pl.BlockSpec((TM_OUT, D), lambda i: (i, 0)),
            _mod_spec(TM_OUT),
            pl.BlockSpec((D, D), lambda i: (0, 0)),
        ],
        out_specs=pl.BlockSpec((TM_OUT, D), lambda i: (i, 0)),
        compiler_params=_cparams("arbitrary"),
        name="na_oproj",
    )(a_ctx, a_lat, x, gate, w)


def _ssd_out_kernel(yc_ref, yl_ref, z_ref, x_ref, gt_ref, ng_ref, w_ref, o_ref):
    def run(y_ref):
        v = y_ref[...] * _silu(z_ref[...])
        v = _rms(v, ng_ref[...]).astype(BF16)
        o_ref[...] = x_ref[...] + gt_ref[...] * _dot(v, w_ref[...])

    is_ctx = pl.program_id(0) < T_CTX // TM_OUT
    pl.when(is_ctx)(lambda: run(yc_ref))
    pl.when(jnp.logical_not(is_ctx))(lambda: run(yl_ref))


def _ssd_out_call(y_ctx, y_lat, zx, x, gate, norm_g, w):
    return pl.pallas_call(
        _ssd_out_kernel,
        out_shape=jax.ShapeDtypeStruct((T_ALL, D), F32),
        grid=(T_ALL // TM_OUT,),
        in_specs=[
            pl.BlockSpec((TM_OUT, D_INNER), lambda i: (_two_stream_rows(i, TM_OUT)[0], 0)),
            pl.BlockSpec((TM_OUT, D_INNER), lambda i: (_two_stream_rows(i, TM_OUT)[1], 0)),
            pl.BlockSpec((TM_OUT, D_INNER), lambda i: (i, 0)),
            pl.BlockSpec((TM_OUT, D), lambda i: (i, 0)),
            _mod_spec(TM_OUT),
            pl.BlockSpec((1, D_INNER), lambda i: (0, 0)),
            pl.BlockSpec((D_INNER, D), lambda i: (0, 0)),
        ],
        out_specs=pl.BlockSpec((TM_OUT, D), lambda i: (i, 0)),
        compiler_params=_cparams("arbitrary"),
        name="ssd_out",
    )(y_ctx, y_lat, zx, x, gate, norm_g, w)


def _ffn_kernel(x_ref, g_ref, sh_ref, sc_ref, gt_ref, wgu_ref, wd_ref, fg_ref, o_ref, *, final_norm):
    half = TM_FFN // 2
    for r in range(2):
        rows = slice(r * half, (r + 1) * half)
        h = _norm_mod(x_ref[rows, :], g_ref[...], sh_ref[...], sc_ref[...]).astype(BF16)
        acc = None
        for j in range(D_FF // TF):
            gate_cols = slice(j * TF, (j + 1) * TF)
            up_cols = slice(D_FF + j * TF, D_FF + (j + 1) * TF)
            act = (_silu(_dot(h, wgu_ref[:, gate_cols])) * _dot(h, wgu_ref[:, up_cols])).astype(BF16)
            part = _dot(act, wd_ref[gate_cols, :])
            acc = part if acc is None else acc + part
        y = x_ref[rows, :] + gt_ref[...] * acc
        if final_norm:
            y = _rms(y, fg_ref[...])
        o_ref[rows, :] = y


def _ffn_call(x, g, shift, scale, gate, w_gu, w_d, final_g, final_norm, tile0=0, n_tiles=T_ALL // TM_FFN):
    resident = dict(pipeline_mode=pl.Buffered(1))
    return pl.pallas_call(
        functools.partial(_ffn_kernel, final_norm=final_norm),
        out_shape=jax.ShapeDtypeStruct((n_tiles * TM_FFN, D), F32),
        grid=(n_tiles,),
        in_specs=[
            pl.BlockSpec((TM_FFN, D), lambda i: (i + tile0, 0)),
            pl.BlockSpec((1, D), lambda i: (0, 0)),
            _mod_spec(TM_FFN, tile0),
            _mod_spec(TM_FFN, tile0),
            _mod_spec(TM_FFN, tile0),
            pl.BlockSpec((D, 2 * D_FF), lambda i: (0, 0), **resident),
            pl.BlockSpec((D_FF, D), lambda i: (0, 0), **resident),
            pl.BlockSpec((1, D), lambda i: (0, 0)),
        ],
        out_specs=pl.BlockSpec((TM_FFN, D), lambda i: (i, 0)),
        compiler_params=_cparams("arbitrary"),
        name="ffn",
    )(x, g, shift, scale, gate, w_gu, w_d, final_g)


def _bias_kernel(rpb_ref, te_ref, to_ref):
    h = pl.program_id(0)
    qc = lax.broadcasted_iota(jnp.int32, (GRID_W, 128), 0)
    lane = lax.broadcasted_iota(jnp.int32, (GRID_W, 128), 1)
    kc = lane & (GRID_W - 1)
    upper = lane >= GRID_W
    idx = jnp.clip(kc - qc + (WIN_W - 1), 0, RPB_W - 1)
    start = jnp.clip(qc - WIN_W // 2, 0, GRID_W - WIN_W)
    in_win = (kc >= start) & (kc < start + WIN_W)

    def pair_tile(dr_lo, dr_hi):
        def ok(dr):
            return 0 <= dr < RPB_H

        acc = jnp.zeros((GRID_W, 128), F32)
        for d in range(RPB_W):
            lo = rpb_ref[h * (RPB_H * RPB_W) + dr_lo * RPB_W + d] if ok(dr_lo) else 0.0
            hi = rpb_ref[h * (RPB_H * RPB_W) + dr_hi * RPB_W + d] if ok(dr_hi) else 0.0
            acc = jnp.where(idx == d, jnp.where(upper, hi, lo), acc)
        if ok(dr_lo) and ok(dr_hi):
            valid = in_win
        elif ok(dr_lo):
            valid = in_win & jnp.logical_not(upper)
        elif ok(dr_hi):
            valid = in_win & upper
        else:
            return jnp.full((GRID_W, 128), NEG_INF, F32)
        return jnp.where(valid, acc, NEG_INF)

    for t in range(BIAS_BLOCKS // 2):
        te_ref[:, t * 128:(t + 1) * 128] = pair_tile(2 * t - 1, 2 * t)
        to_ref[:, t * 128:(t + 1) * 128] = pair_tile(2 * t - 2, 2 * t - 1)


def _bias_call(rpb):
    shape = jax.ShapeDtypeStruct((HEADS, GRID_W, BIAS_BLOCKS * GRID_W), F32)
    spec = pl.BlockSpec((None, GRID_W, BIAS_BLOCKS * GRID_W), lambda h: (h, 0, 0))
    return pl.pallas_call(
        _bias_kernel,
        out_shape=(shape, shape),
        grid=(HEADS,),
        in_specs=[pl.BlockSpec(memory_space=pltpu.SMEM)],
        out_specs=(spec, spec),
        compiler_params=_cparams("arbitrary"),
        name="na_bias",
    )(rpb.reshape(-1))


def _half_mask(e):
    lane = lax.broadcasted_iota(jnp.int32, (1, 128), 1)
    return (lane >= HEAD_DIM) if e else (lane < HEAD_DIM)


def _na_ctx_kernel(qkv_ref, *refs):
    o_ref, ko_ref, vo_ref = refs[-3:]
    scale = HEAD_DIM ** -0.5
    for hp in range(HEAD_PAIRS):
        q2 = qkv_ref[:, hp * 128:(hp + 1) * 128] * scale
        k2 = qkv_ref[:, D + hp * 128:D + (hp + 1) * 128]
        v2 = qkv_ref[:, 2 * D + hp * 128:2 * D + (hp + 1) * 128]
        for e in range(2):
            ko_ref[2 * hp + e] = k2[:, e * HEAD_DIM:(e + 1) * HEAD_DIM]
            vo_ref[2 * hp + e] = v2[:, e * HEAD_DIM:(e + 1) * HEAD_DIM]
        kb = k2.astype(BF16)
        vb = v2.astype(BF16)
        outs = []
        for e in range(2):
            qm = jnp.where(_half_mask(e), q2, 0.0).astype(BF16)
            s = _dot_nt(qm, kb)
            p = jnp.exp(s - jnp.max(s, axis=-1, keepdims=True))
            l = jnp.sum(p, axis=-1, keepdims=True)
            outs.append(_dot(p.astype(BF16), vb) / l)
        o_ref[:, hp * 128:(hp + 1) * 128] = jnp.where(_half_mask(0), outs[0], outs[1]).astype(BF16)


def _na_ctx_call(qkv, kv_prev, layer):
    kv_shape = jax.ShapeDtypeStruct((N_CTX, (DEPTH + 1) // 2, HEADS, L_CTX, HEAD_DIM), F32)
    kv_spec = pl.BlockSpec((None, None, HEADS, L_CTX, HEAD_DIM), lambda n: (n, layer, 0, 0, 0))
    in_specs = [pl.BlockSpec((L_CTX, 3 * D), lambda n: (n, 0))]
    args = [qkv]
    aliases = {}
    if kv_prev is not None:
        in_specs += [pl.BlockSpec(memory_space=pl.ANY)] * 2
        args += list(kv_prev)
        aliases = {1: 1, 2: 2}
    return pl.pallas_call(
        _na_ctx_kernel,
        out_shape=(jax.ShapeDtypeStruct((T_CTX, D), BF16), kv_shape, kv_shape),
        grid=(N_CTX,),
        in_specs=in_specs,
        out_specs=(pl.BlockSpec((L_CTX, D), lambda n: (n, 0)), kv_spec, kv_spec),
        input_output_aliases=aliases,
        compiler_params=_cparams("arbitrary"),
        name="na_ctx",
    )(*args)


def _window_of_row(r):
    r0 = min(max(r - WIN_H // 2, 0), GRID_H - WIN_H)
    r0a = r0 & ~1
    rows = WIN_H if r0a == r0 else WIN_H + 2
    return r0, r0a, rows


def _na_lat_kernel(q_ref, k_ref, v_ref, kc_ref, vc_ref, te_ref, to_ref, o_ref, *scratch):
    @pl.when((pl.program_id(0) == 0) & (pl.program_id(1) == 0))
    def _():
        scratch[2][...] = jnp.zeros_like(scratch[2])

    q2 = q_ref[...] * (HEAD_DIM ** -0.5)
    kb = k_ref[...].astype(BF16)
    vb = v_ref[...].astype(BF16)
    kcb = kc_ref[...].astype(BF16)
    vcb = vc_ref[...].astype(BF16)
    outs = []
    for e in range(2):
        s_scr, sc_scr, p_scr, pc_scr, li_scr = (ref.at[e] for ref in scratch)
        qm = jnp.where(_half_mask(e), q2, 0.0).astype(BF16)
        s_scr[...] = _dot_nt(qm, kb)
        sc_scr[...] = _dot_nt(qm, kcb)
        for r in range(GRID_H):
            r0, r0a, nrows = _window_of_row(r)
            width = nrows * GRID_W
            blk = r0a - r + WIN_H
            t_ref, t_off = (te_ref, blk * GRID_W) if r % 2 == 0 else (to_ref, (blk + 1) * GRID_W)
            rows = slice(r * GRID_W, (r + 1) * GRID_W)
            cols = slice(r0a * GRID_W, r0a * GRID_W + width)
            sl = s_scr[rows, cols] + t_ref[e, :, t_off:t_off + width]
            if nrows != WIN_H:
                lane = lax.broadcasted_iota(jnp.int32, (1, width), 1)
                sl = jnp.where((lane < GRID_W) | (lane >= width - GRID_W), NEG_INF, sl)
            sc = sc_scr[rows, :]
            m = jnp.maximum(jnp.max(sl, axis=-1, keepdims=True), jnp.max(sc, axis=-1, keepdims=True))
            p_loc = jnp.exp(sl - m)
            p_ctx = jnp.exp(sc - m)
            l = jnp.sum(p_loc, axis=-1, keepdims=True) + jnp.sum(p_ctx, axis=-1, keepdims=True)
            p_scr[rows, cols] = p_loc.astype(BF16)
            pc_scr[rows, :] = p_ctx.astype(BF16)
            li_scr[rows, :] = jnp.broadcast_to(1.0 / l, (GRID_W, 128))
        outs.append((_dot(p_scr[...], vb) + _dot(pc_scr[...], vcb)) * li_scr[...])
    o_ref[...] = jnp.where(_half_mask(0), outs[0], outs[1]).astype(BF16)


def _na_lat_call(qkv, kc_pairs, vc_pairs, t_even, t_odd):
    row0 = T_CTX // L_LAT
    ctx_spec = pl.BlockSpec((None, None, PAST, 128), lambda b, hp: (b, hp, 0, 0))
    t_spec = pl.BlockSpec((2, GRID_W, BIAS_BLOCKS * GRID_W), lambda b, hp: (hp, 0, 0))
    return pl.pallas_call(
        _na_lat_kernel,
        out_shape=jax.ShapeDtypeStruct((T_LAT, D), BF16),
        grid=(N_LAT, HEAD_PAIRS),
        in_specs=[
            pl.BlockSpec((L_LAT, 128), lambda b, hp: (row0 + b, hp)),
            pl.BlockSpec((L_LAT, 128), lambda b, hp: (row0 + b, HEAD_PAIRS + hp)),
            pl.BlockSpec((L_LAT, 128), lambda b, hp: (row0 + b, 2 * HEAD_PAIRS + hp)),
            ctx_spec, ctx_spec, t_spec, t_spec,
        ],
        out_specs=pl.BlockSpec((L_LAT, 128), lambda b, hp: (b, hp)),
        scratch_shapes=[
            pltpu.VMEM((2, L_LAT, L_LAT), F32),
            pltpu.VMEM((2, L_LAT, PAST), F32),
            pltpu.VMEM((2, L_LAT, L_LAT), BF16),
            pltpu.VMEM((2, L_LAT, PAST), BF16),
            pltpu.VMEM((2, L_LAT, 128), F32),
        ],
        compiler_params=_cparams("arbitrary", "arbitrary"),
        name="na_lat",
    )(qkv, qkv, qkv, kc_pairs, vc_pairs, t_even, t_odd)


def _pair_heads(cache):
    b, h, l, hd = cache.shape
    return cache.reshape(b, h // 2, 2, l, hd).transpose(0, 1, 3, 2, 4).reshape(b, h // 2, l, 2 * hd)


def _ssd_kernel(*refs, seq, gs, has_init, has_final, has_prev):
    it = iter(refs)
    x_ref, b_ref, c_ref, dt_ref, bias_ref, nega_ref, d_ref = (next(it) for _ in range(7))
    init_ref = next(it) if has_init else None
    if has_prev:
        next(it)
    y_ref = next(it)
    fin_ref = next(it) if has_final else None
    sloc_scr, sin_scr = next(it), next(it)

    nc = seq // CHUNK
    g0 = pl.program_id(1) * gs
    rows = SSD_R * SSD_P

    src = lax.broadcasted_iota(jnp.int32, (128, 128), 0)
    dst = lax.broadcasted_iota(jnp.int32, (128, 128), 1)
    want = ((dst >> 2) & 1) * SSD_HEADS + (g0 + (dst >> 3)) * SSD_R + (dst & 3)
    sel = jnp.where((dst < 8 * gs) & (src == want), 1.0, 0.0).astype(BF16)
    dt_raw = sum(_dot(p, sel) for p in _split3(dt_ref[...]))
    u = dt_raw + bias_ref[...]
    dt = jnp.maximum(u, 0.0) + jnp.log1p(jnp.exp(-jnp.abs(u)))
    a = dt * nega_ref[...]

    li = lax.broadcasted_iota(jnp.int32, (CHUNK, CHUNK), 0)
    si = lax.broadcasted_iota(jnp.int32, (CHUNK, CHUNK), 1)
    tril = jnp.where(si <= li, 1.0, 0.0).astype(BF16)
    triu = jnp.where(si >= li, 1.0, 0.0).astype(BF16)
    pre, suf = [], []
    for c in range(nc):
        parts = _split3(a[c * CHUNK:(c + 1) * CHUNK])
        pre.append(sum(_dot(tril, p) for p in parts))
        suf.append(sum(_dot(triu, p) for p in parts))
    pre = jnp.concatenate(pre, axis=0)
    suf = jnp.concatenate(suf, axis=0)
    fwd_lane = (lax.broadcasted_iota(jnp.int32, (1, 128), 1) & SSD_R) == 0
    e1 = jnp.where(fwd_lane, pre, suf)
    e2 = jnp.where(fwd_lane, suf, pre) - a
    e1s = e1 * LOG2E
    e1s_t = e1s.T
    dt_t = dt.T
    ysc_t = jnp.exp(e1).T
    w2_t = jnp.exp(e2).T

    def expand(t, off, cols):
        return jnp.concatenate(
            [jnp.broadcast_to(t[off + h:off + h + 1, cols], (SSD_P, cols.stop - cols.start))
             for h in range(SSD_R)], axis=0)

    def chunk_decay(off, col):
        return expand(ysc_t, off, slice(col, col + 1))

    x_t = x_ref[...].T

    for gi in range(gs):
        fwd, bwd = 8 * gi, 8 * gi + SSD_R
        xg_t = x_t[gi * rows:(gi + 1) * rows]
        b_cols = slice(gi * SSD_N, (gi + 1) * SSD_N)
        y_cols = slice(gi * rows, (gi + 1) * rows)

        for c in range(nc):
            cols = slice(c * CHUNK, (c + 1) * CHUNK)
            xc = xg_t[:, cols]
            xw = jnp.concatenate([xc * (expand(dt_t, fwd, cols) * expand(w2_t, fwd, cols)),
                                  xc * (expand(dt_t, bwd, cols) * expand(w2_t, bwd, cols))], axis=0)
            sloc_scr[gi, c] = _dot(xw.astype(BF16), b_ref[cols, b_cols].astype(BF16))

        heads = slice(gi * SSD_R, (gi + 1) * SSD_R)
        s_f = init_ref[0, heads].reshape(rows, SSD_N) if has_init else None
        for c in range(nc):
            if s_f is not None:
                sin_scr[gi, c, 0:rows] = s_f
                s_f = s_f * chunk_decay(fwd, c * CHUNK + CHUNK - 1) + sloc_scr[gi, c, 0:rows]
            else:
                s_f = sloc_scr[gi, c, 0:rows]
        s_b = init_ref[1, heads].reshape(rows, SSD_N) if has_init else None
        for c in reversed(range(nc)):
            if s_b is not None:
                sin_scr[gi, c, rows:2 * rows] = s_b
                s_b = s_b * chunk_decay(bwd, c * CHUNK) + sloc_scr[gi, c, rows:2 * rows]
            else:
                s_b = sloc_scr[gi, c, rows:2 * rows]
        if has_final:
            fin_ref[0, heads] = s_f.reshape(SSD_R, SSD_P, SSD_N)
            fin_ref[1, heads] = s_b.reshape(SSD_R, SSD_P, SSD_N)

        for c in range(nc):
            cols = slice(c * CHUNK, (c + 1) * CHUNK)
            bc = b_ref[cols, b_cols].astype(BF16)
            cc = c_ref[cols, b_cols].astype(BF16)
            cb_t = _dot_nt(bc, cc)
            xc = xg_t[:, cols]
            xdf = xc * expand(dt_t, fwd, cols)
            xdb = xc * expand(dt_t, bwd, cols)
            entering = []
            if has_init or c > 0:
                entering.append((slice(0, rows), fwd))
            if has_init or c < nc - 1:
                entering.append((slice(rows, 2 * rows), bwd))
            off = None
            if entering:
                s_in = jnp.concatenate([sin_scr[gi, c, r] for r, _ in entering], axis=0)
                prod = _dot_nt(s_in.astype(BF16), cc)
                for k, (_, base) in enumerate(entering):
                    term = prod[k * rows:(k + 1) * rows] * expand(ysc_t, base, cols)
                    off = term if off is None else off + term
            y_heads = []
            for h in range(SSD_R):
                pf_col = e1s[cols, fwd + h:fwd + h + 1]
                rb_col = e1s[cols, bwd + h:bwd + h + 1]
                pf_row = e1s_t[fwd + h:fwd + h + 1, cols]
                rb_row = e1s_t[bwd + h:bwd + h + 1, cols]
                m_f = jnp.exp2(jnp.where(li <= si, pf_row - pf_col, NEG_INF)) * cb_t
                m_b = jnp.exp2(jnp.where(li >= si, rb_row - rb_col, NEG_INF)) * cb_t
                hs = slice(h * SSD_P, (h + 1) * SSD_P)
                lhs = jnp.concatenate([xdf[hs], xdb[hs]], axis=1).astype(BF16)
                rhs = jnp.concatenate([m_f, m_b], axis=0).astype(BF16)
                y_h = _dot(lhs, rhs)
                y_heads.append(y_h if off is None else y_h + off[hs])
            y_t = jnp.concatenate(y_heads, axis=0)
            y_ref[cols, y_cols] = y_t.T + x_ref[cols, y_cols] * d_ref[:, y_cols]


def _ssd_call(zx, dt, dt_bias, a_log, d, state, fin_prev, layer, *, latent):
    seq = L_LAT if latent else L_CTX
    n_seq = N_LAT if latent else N_CTX
    gs = GS_LAT if latent else GS_CTX
    row0 = T_CTX // seq if latent else 0
    nc = seq // CHUNK
    xw, bw = gs * SSD_R * SSD_P, gs * SSD_N
    xcol0 = D_INNER // xw
    bcol0 = (2 * D_INNER) // bw
    ccol0 = bcol0 + SSD_GROUPS // gs
    step_spec = lambda w: pl.BlockSpec((None, 1, w), lambda n, k: (k, 0, 0))
    in_specs = [
        pl.BlockSpec((seq, xw), lambda n, k: (row0 + n, xcol0 + k)),
        pl.BlockSpec((seq, bw), lambda n, k: (row0 + n, bcol0 + k)),
        pl.BlockSpec((seq, bw), lambda n, k: (row0 + n, ccol0 + k)),
        pl.BlockSpec((seq, 128), lambda n, k: (row0 + n, 0)),
        step_spec(128), step_spec(128), step_spec(xw),
    ]
    args = [zx, zx, zx, dt, _step_lanes(dt_bias, gs), _step_lanes(-jnp.exp(a_log), gs),
            jnp.repeat(d, SSD_P).reshape(SSD_GROUPS // gs, 1, xw)]
    state_block = (None, None, 2, gs * SSD_R, SSD_P, SSD_N)
    aliases = {}
    if latent:
        in_specs.append(pl.BlockSpec(state_block, lambda n, k: (n, layer, 0, k, 0, 0)))
        args.append(state)
    elif fin_prev is not None:
        in_specs.append(pl.BlockSpec(memory_space=pl.ANY))
        args.append(fin_prev)
        aliases = {len(args) - 1: 1}
    y_shape = jax.ShapeDtypeStruct((n_seq * seq, D_INNER), F32)
    y_spec = pl.BlockSpec((seq, xw), lambda n, k: (n, k))
    if latent:
        out_shape, out_specs = y_shape, y_spec
    else:
        out_shape = (y_shape, jax.ShapeDtypeStruct((N_CTX, DEPTH // 2, 2, SSD_HEADS, SSD_P, SSD_N), F32))
        out_specs = (y_spec, pl.BlockSpec(state_block, lambda n, k: (n, layer, 0, k, 0, 0)))
    scratch = pltpu.VMEM((gs, nc, 2 * SSD_R * SSD_P, SSD_N), F32)
    return pl.pallas_call(
        functools.partial(_ssd_kernel, seq=seq, gs=gs, has_init=latent, has_final=not latent,
                          has_prev=bool(aliases)),
        out_shape=out_shape,
        grid=(n_seq, SSD_GROUPS // gs),
        in_specs=in_specs,
        out_specs=out_specs,
        input_output_aliases=aliases,
        scratch_shapes=[scratch, scratch],
        compiler_params=_cparams("arbitrary", "arbitrary"),
        name="ssd_scan_lat" if latent else "ssd_scan_ctx",
    )(*args)


def _step_lanes(p, gs):
    q = p.reshape(2, SSD_GROUPS, SSD_R).transpose(1, 0, 2).reshape(SSD_GROUPS // gs, 8 * gs)
    return jnp.pad(q, ((0, 0), (0, 128 - 8 * gs))).reshape(SSD_GROUPS // gs, 1, 128)


def kernel(x_prompt, x_sample, cache_k, cache_v, state_ssm, c, c_ctx, ada_w, ada_b, norm_mix_g, norm_ffn_g,
           ffn_w_gate_up, ffn_w_down, na_w_qkv, na_w_o, na_rpb, ssd_w_in, ssd_conv_w, ssd_conv_b,
           ssd_dt_bias, ssd_a_log, ssd_d, ssd_norm_g, ssd_w_out, final_norm_g):
    cond = jnp.zeros((N_COND, D), F32).at[0].set(c_ctx).at[1:1 + N_LAT].set(c)
    mods = _ada_call(cond, ada_w, ada_b).reshape(DEPTH, N_COND, 6, 1, D)
    x = jnp.concatenate([x_prompt.reshape(T_CTX, D), x_sample.reshape(T_LAT, D)], axis=0)
    final_g = final_norm_g.reshape(1, D)

    new_kv, new_s = None, None
    for i in range(DEPTH):
        j = i // 2
        m = [mods[i, :, t] for t in range(6)]
        g_mix = norm_mix_g[i].reshape(1, D)
        if i % 2 == 0:
            qkv = _qkv_call(x, g_mix, m[0], m[1], na_w_qkv[j].astype(BF16))
            o_ctx, *new_kv = _na_ctx_call(qkv, new_kv, j)
            t_even, t_odd = _bias_call(na_rpb[j])
            o_lat = _na_lat_call(qkv, _pair_heads(cache_k[:, j]), _pair_heads(cache_v[:, j]), t_even, t_odd)
            x = _oproj_call(o_ctx, o_lat, x, m[2], na_w_o[j].astype(BF16))
        else:
            w_in = ssd_w_in[j].astype(BF16)
            w_dt = jnp.pad(w_in[:, ZX_DIM:], ((0, 0), (0, 128 - 2 * SSD_HEADS)))
            conv_w = jnp.pad(ssd_conv_w[j], ((0, 0), (D_INNER, 0)))
            conv_b = jnp.pad(ssd_conv_b[j], (D_INNER, 0)).reshape(1, ZX_DIM)
            zx, dt = _inproj_call(x, g_mix, m[0], m[1], w_in[:, :ZX_DIM], w_dt, conv_w, conv_b)
            scan_args = (zx, dt, ssd_dt_bias[j], ssd_a_log[j], ssd_d[j])
            y_ctx, new_s = _ssd_call(*scan_args, None, new_s, j, latent=False)
            y_lat = _ssd_call(*scan_args, state_ssm, None, j, latent=True)
            x = _ssd_out_call(y_ctx, y_lat, zx, x, m[2], ssd_norm_g[j].reshape(1, D_INNER),
                              ssd_w_out[j].astype(BF16))
        ffn_args = (x, norm_ffn_g[i].reshape(1, D), m[3], m[4], m[5],
                    ffn_w_gate_up[i].astype(BF16), ffn_w_down[i].astype(BF16), final_g)
        if i < DEPTH - 1:
            x = _ffn_call(*ffn_args, final_norm=False)

    ctx_tiles = T_CTX // TM_FFN
    y_prompt = _ffn_call(*ffn_args, final_norm=True, tile0=0, n_tiles=ctx_tiles)
    y_sample = _ffn_call(*ffn_args, final_norm=True, tile0=ctx_tiles, n_tiles=T_LAT // TM_FFN)
    return (y_prompt.reshape(N_CTX, L_CTX, D), y_sample.reshape(N_LAT, L_LAT, D), new_kv[0], new_kv[1], new_s)
```

```python
import functools

import jax
import jax.numpy as jnp
from jax import lax
from jax.experimental import pallas as pl
from jax.experimental.pallas import tpu as pltpu

F32 = jnp.float32
BF16 = jnp.bfloat16

D = 1024
N_CTX, L_CTX = 32, 256
N_LAT, L_LAT = 4, 1024
T_CTX = N_CTX * L_CTX
T_LAT = N_LAT * L_LAT
T_ALL = T_CTX + T_LAT
DEPTH = 4
N_COND = 8
PAST = 512
GRID_W = 64
GRID_H = L_LAT // GRID_W
HEADS = 16
HEAD_DIM = 64
HEAD_PAIRS = HEADS // 2
WIN_H, WIN_W = 8, 16
RPB_H, RPB_W = 2 * WIN_H - 1, 2 * WIN_W - 1
D_INNER = 2048
SSD_HEADS = 32
SSD_P = 64
SSD_GROUPS = 8
SSD_R = SSD_HEADS // SSD_GROUPS
SSD_N = 128
CHUNK = 128
CONV_DIM = D_INNER + 2 * SSD_GROUPS * SSD_N
ZX_DIM = D_INNER + CONV_DIM
D_FF = 2816
EPS = 1e-6
NEG_INF = float("-inf")
LOG2E = 1.4426950408889634

TM = 1024
TM_FFN = 1024
TF = D_FF // 2
TN_QKV = 1024
TN_IN = 1024
MM_COLS = 512
TM_OUT = 512
GS_CTX = 8
GS_LAT = 2
BIAS_BLOCKS = 18
VMEM_LIMIT = 56 * 1024 * 1024


def _cparams(*sem):
    return pltpu.CompilerParams(dimension_semantics=sem, vmem_limit_bytes=VMEM_LIMIT)


def _cond_of_tile(i, tm):
    return jnp.maximum((i * tm) // L_LAT - (T_CTX // L_LAT - 1), 0)


def _silu(x):
    return x * (0.5 * jnp.tanh(0.5 * x) + 0.5)


def _rms(x, g):
    return x * lax.rsqrt(jnp.mean(x * x, axis=-1, keepdims=True) + EPS) * g


def _norm_mod(x, g, shift, scale):
    return _rms(x, g) * (1.0 + scale) + shift


def _dot(a, b):
    return jnp.dot(a, b, preferred_element_type=F32)


def _dot_nt(a, b):
    return lax.dot_general(a, b, (((1,), (1,)), ((), ())), preferred_element_type=F32)


def _split3(x):
    hi = x.astype(BF16)
    r = x - hi.astype(F32)
    mid = r.astype(BF16)
    lo = (r - mid.astype(F32)).astype(BF16)
    return hi, mid, lo


def _ada_kernel(c_ref, w_ref, b_ref, o_ref):
    s = _silu(c_ref[...]).astype(BF16)
    o_ref[...] = _dot(s, w_ref[...].astype(BF16)) + b_ref[...]


def _ada_call(cond, ada_w, ada_b):
    tn = 1536
    return pl.pallas_call(
        _ada_kernel,
        out_shape=jax.ShapeDtypeStruct((DEPTH, N_COND, 6 * D), F32),
        grid=(DEPTH, 6 * D // tn),
        in_specs=[
            pl.BlockSpec((N_COND, D), lambda l, j: (0, 0)),
            pl.BlockSpec((None, D, tn), lambda l, j: (l, 0, j)),
            pl.BlockSpec((None, 1, tn), lambda l, j: (l, 0, j)),
        ],
        out_specs=pl.BlockSpec((None, N_COND, tn), lambda l, j: (l, 0, j)),
        compiler_params=_cparams("arbitrary", "arbitrary"),
        name="ada",
    )(cond, ada_w, ada_b.reshape(DEPTH, 1, 6 * D))


def _mod_spec(tm, tile0=0):
    return pl.BlockSpec((None, 1, D), lambda i, *_: (_cond_of_tile(i + tile0, tm), 0, 0))


def _stream_specs(streams, tm, width):
    n_ctx = T_CTX // tm
    lat0 = 0 if isinstance(streams, tuple) else n_ctx
    arrays = streams if isinstance(streams, tuple) else (streams, streams)
    specs = [pl.BlockSpec((tm, width), lambda i, *_: (jnp.minimum(i, n_ctx - 1), 0)),
             pl.BlockSpec((tm, width), lambda i, *_: (lat0 + jnp.maximum(i - n_ctx, 0), 0))]
    return list(arrays), specs


def _on_stream(tm, fn, ctx_ref, lat_ref, extra=True):
    is_ctx = pl.program_id(0) < T_CTX // tm
    pl.when(is_ctx & extra)(lambda: fn(ctx_ref))
    pl.when(jnp.logical_not(is_ctx) & extra)(lambda: fn(lat_ref))


def _qkv_kernel(xc_ref, xl_ref, g_ref, sh_ref, sc_ref, w_ref, o_ref, h_scr):
    def prologue(x_ref):
        h_scr[...] = _norm_mod(x_ref[...], g_ref[...], sh_ref[...], sc_ref[...]).astype(BF16)

    _on_stream(TM, prologue, xc_ref, xl_ref, pl.program_id(1) == 0)
    o_ref[...] = _dot(h_scr[...], w_ref[...])


def _qkv_call(x, g, shift, scale, w_all, layer):
    n = w_all.shape[2]
    x_args, x_specs = _stream_specs(x, TM, D)
    return pl.pallas_call(
        _qkv_kernel,
        out_shape=jax.ShapeDtypeStruct((T_ALL, n), F32),
        grid=(T_ALL // TM, n // TN_QKV),
        in_specs=x_specs + [
            pl.BlockSpec((1, D), lambda i, j: (0, 0)),
            _mod_spec(TM),
            _mod_spec(TM),
            pl.BlockSpec((None, D, TN_QKV), lambda i, j: (layer, 0, j)),
        ],
        out_specs=pl.BlockSpec((TM, TN_QKV), lambda i, j: (i, j)),
        scratch_shapes=[pltpu.VMEM((TM, D), BF16)],
        compiler_params=_cparams("arbitrary", "arbitrary"),
        name="na_qkv",
    )(*x_args, g, shift, scale, w_all)


def _inproj_kernel(x_ref, g_ref, sh_ref, sc_ref, w_ref, wdt_ref, cw_ref, cb_ref, z_ref, xbc_ref, dt_ref,
                   h_scr, c_scr):
    i = pl.program_id(0)
    s = pl.program_id(1)
    n_conv = CONV_DIM // TN_IN

    @pl.when((i == 0) & (s == 0))
    def _():
        c_scr[0:8, :] = jnp.zeros((8, TN_IN), F32)
        c_scr[TM + 8:TM + 16, :] = jnp.zeros((8, TN_IN), F32)

    @pl.when(s == 0)
    def _():
        h = _norm_mod(x_ref[...], g_ref[...], sh_ref[...], sc_ref[...]).astype(BF16)
        h_scr[...] = h
        dt_ref[...] = _dot(h, wdt_ref[...])

    @pl.when(s >= n_conv)
    def _():
        z_ref[...] = _dot(h_scr[...], w_ref[...])

    @pl.when(s < n_conv)
    def _():
        seq = jnp.where(i < T_CTX // TM, L_CTX, L_LAT)
        pos = lax.broadcasted_iota(jnp.int32, (TM, 1), 0) & (seq - 1)
        for m0 in range(0, TN_IN, MM_COLS):
            cols = slice(m0, m0 + MM_COLS)
            acc = _dot(h_scr[...], w_ref[:, cols])
            c_scr[8:TM + 8, cols] = acc
            prev = jnp.where(pos == 0, 0.0, c_scr[7:TM + 7, cols])
            nxt = jnp.where(pos == seq - 1, 0.0, c_scr[9:TM + 9, cols])
            y = cw_ref[0:1, cols] * prev + cw_ref[1:2, cols] * acc + cw_ref[2:3, cols] * nxt + cb_ref[:, cols]
            xbc_ref[:, cols] = _silu(y)


def _inproj_call(x, g, shift, scale, w_all, w_dt, conv_w_all, conv_b_all, layer):
    n_conv = CONV_DIM // TN_IN
    n_steps = ZX_DIM // TN_IN
    conv_tile = lambda s: jnp.minimum(s, n_conv - 1)
    return pl.pallas_call(
        _inproj_kernel,
        out_shape=(jax.ShapeDtypeStruct((T_ALL, D_INNER), F32),
                   jax.ShapeDtypeStruct((T_ALL, CONV_DIM), F32),
                   jax.ShapeDtypeStruct((T_ALL, 128), F32)),
        grid=(T_ALL // TM, n_steps),
        in_specs=[
            pl.BlockSpec((TM, D), lambda i, s: (i, 0)),
            pl.BlockSpec((1, D), lambda i, s: (0, 0)),
            _mod_spec(TM),
            _mod_spec(TM),
            pl.BlockSpec((None, D, TN_IN), lambda i, s: (layer, 0, (s + D_INNER // TN_IN) % n_steps)),
            pl.BlockSpec((D, 128), lambda i, s: (0, 0)),
            pl.BlockSpec((None, 3, TN_IN), lambda i, s: (layer, 0, conv_tile(s))),
            pl.BlockSpec((None, 1, TN_IN), lambda i, s: (layer, 0, conv_tile(s))),
        ],
        out_specs=(pl.BlockSpec((TM, TN_IN), lambda i, s: (i, jnp.maximum(s - n_conv, 0))),
                   pl.BlockSpec((TM, TN_IN), lambda i, s: (i, conv_tile(s))),
                   pl.BlockSpec((TM, 128), lambda i, s: (i, 0))),
        scratch_shapes=[pltpu.VMEM((TM, D), BF16), pltpu.VMEM((TM + 16, TN_IN), F32)],
        compiler_params=_cparams("arbitrary", "arbitrary"),
        name="ssd_inproj",
    )(x, g, shift, scale, w_all, w_dt, conv_w_all, conv_b_all)


def _oproj_kernel(ac_ref, al_ref, xc_ref, xl_ref, gt_ref, w_ref, o_ref):
    def run(refs):
        a_ref, x_ref = refs
        o_ref[...] = x_ref[...] + gt_ref[...] * _dot(a_ref[...], w_ref[...])

    _on_stream(TM_OUT, run, (ac_ref, xc_ref), (al_ref, xl_ref))


def _oproj_call(a_ctx, a_lat, x, gate, w_all, layer):
    a_args, a_specs = _stream_specs((a_ctx, a_lat), TM_OUT, D)
    x_args, x_specs = _stream_specs(x, TM_OUT, D)
    return pl.pallas_call(
        _oproj_kernel,
        out_shape=jax.ShapeDtypeStruct((T_ALL, D), F32),
        grid=(T_ALL // TM_OUT,),
        in_specs=a_specs + x_specs + [
            _mod_spec(TM_OUT),
            pl.BlockSpec((None, D, D), lambda i: (layer, 0, 0)),
        ],
        out_specs=pl.BlockSpec((TM_OUT, D), lambda i: (i, 0)),
        compiler_params=_cparams("arbitrary"),
        name="na_oproj",
    )(*a_args, *x_args, gate, w_all)


def _ssd_out_kernel(yc_ref, yl_ref, z_ref, x_ref, gt_ref, ng_ref, w_ref, o_ref):
    def run(y_ref):
        v = y_ref[...] * _silu(z_ref[...])
        v = _rms(v, ng_ref[...]).astype(BF16)
        o_ref[...] = x_ref[...] + gt_ref[...] * _dot(v, w_ref[...])

    _on_stream(TM_OUT, run, yc_ref, yl_ref)


def _ssd_out_call(y_ctx, y_lat, z, x, gate, norm_g, w_all, layer):
    y_args, y_specs = _stream_specs((y_ctx, y_lat), TM_OUT, D_INNER)
    return pl.pallas_call(
        _ssd_out_kernel,
        out_shape=jax.ShapeDtypeStruct((T_ALL, D), F32),
        grid=(T_ALL // TM_OUT,),
        in_specs=y_specs + [
            pl.BlockSpec((TM_OUT, D_INNER), lambda i: (i, 0)),
            pl.BlockSpec((TM_OUT, D), lambda i: (i, 0)),
            _mod_spec(TM_OUT),
            pl.BlockSpec((1, D_INNER), lambda i: (0, 0)),
            pl.BlockSpec((None, D_INNER, D), lambda i: (layer, 0, 0)),
        ],
        out_specs=pl.BlockSpec((TM_OUT, D), lambda i: (i, 0)),
        compiler_params=_cparams("arbitrary"),
        name="ssd_out",
    )(*y_args, z, x, gate, norm_g, w_all)


def _ffn_kernel(x_ref, g_ref, sh_ref, sc_ref, gt_ref, wgu_ref, wd_ref, fg_ref, o_ref, *, final_norm):
    half = TM_FFN // 2
    for r in range(2):
        rows = slice(r * half, (r + 1) * half)
        h = _norm_mod(x_ref[rows, :], g_ref[...], sh_ref[...], sc_ref[...]).astype(BF16)
        acc = None
        for j in range(D_FF // TF):
            gate_cols = slice(j * TF, (j + 1) * TF)
            up_cols = slice(D_FF + j * TF, D_FF + (j + 1) * TF)
            act = (_silu(_dot(h, wgu_ref[:, gate_cols])) * _dot(h, wgu_ref[:, up_cols])).astype(BF16)
            part = _dot(act, wd_ref[gate_cols, :])
            acc = part if acc is None else acc + part
        y = x_ref[rows, :] + gt_ref[...] * acc
        if final_norm:
            y = _rms(y, fg_ref[...])
        o_ref[rows, :] = y


def _ffn_call(x, g, shift, scale, gate, w_gu_all, w_d_all, final_g, layer, final_norm,
              tile0=0, n_tiles=T_ALL // TM_FFN):
    resident = dict(pipeline_mode=pl.Buffered(1))
    return pl.pallas_call(
        functools.partial(_ffn_kernel, final_norm=final_norm),
        out_shape=jax.ShapeDtypeStruct((n_tiles * TM_FFN, D), F32),
        grid=(n_tiles,),
        in_specs=[
            pl.BlockSpec((TM_FFN, D), lambda i: (i + tile0, 0)),
            pl.BlockSpec((1, D), lambda i: (0, 0)),
            _mod_spec(TM_FFN, tile0),
            _mod_spec(TM_FFN, tile0),
            _mod_spec(TM_FFN, tile0),
            pl.BlockSpec((None, D, 2 * D_FF), lambda i: (layer, 0, 0), **resident),
            pl.BlockSpec((None, D_FF, D), lambda i: (layer, 0, 0), **resident),
            pl.BlockSpec((1, D), lambda i: (0, 0)),
        ],
        out_specs=pl.BlockSpec((TM_FFN, D), lambda i: (i, 0)),
        compiler_params=_cparams("arbitrary"),
        name="ffn",
    )(x, g, shift, scale, gate, w_gu_all, w_d_all, final_g)


def _bias_kernel(rpb_ref, te_ref, to_ref):
    h = pl.program_id(0)
    qc = lax.broadcasted_iota(jnp.int32, (GRID_W, 128), 0)
    lane = lax.broadcasted_iota(jnp.int32, (GRID_W, 128), 1)
    kc = lane & (GRID_W - 1)
    upper = lane >= GRID_W
    idx = jnp.clip(kc - qc + (WIN_W - 1), 0, RPB_W - 1)
    start = jnp.clip(qc - WIN_W // 2, 0, GRID_W - WIN_W)
    in_win = (kc >= start) & (kc < start + WIN_W)

    def pair_tile(dr_lo, dr_hi):
        def ok(dr):
            return 0 <= dr < RPB_H

        acc = jnp.zeros((GRID_W, 128), F32)
        for d in range(RPB_W):
            lo = rpb_ref[h * (RPB_H * RPB_W) + dr_lo * RPB_W + d] if ok(dr_lo) else 0.0
            hi = rpb_ref[h * (RPB_H * RPB_W) + dr_hi * RPB_W + d] if ok(dr_hi) else 0.0
            acc = jnp.where(idx == d, jnp.where(upper, hi, lo), acc)
        if ok(dr_lo) and ok(dr_hi):
            valid = in_win
        elif ok(dr_lo):
            valid = in_win & jnp.logical_not(upper)
        elif ok(dr_hi):
            valid = in_win & upper
        else:
            return jnp.full((GRID_W, 128), NEG_INF, F32)
        return jnp.where(valid, acc, NEG_INF)

    for t in range(BIAS_BLOCKS // 2):
        te_ref[:, t * 128:(t + 1) * 128] = pair_tile(2 * t - 1, 2 * t)
        to_ref[:, t * 128:(t + 1) * 128] = pair_tile(2 * t - 2, 2 * t - 1)


def _bias_call(rpb):
    shape = jax.ShapeDtypeStruct((HEADS, GRID_W, BIAS_BLOCKS * GRID_W), F32)
    spec = pl.BlockSpec((None, GRID_W, BIAS_BLOCKS * GRID_W), lambda h: (h, 0, 0))
    return pl.pallas_call(
        _bias_kernel,
        out_shape=(shape, shape),
        grid=(HEADS,),
        in_specs=[pl.BlockSpec(memory_space=pltpu.SMEM)],
        out_specs=(spec, spec),
        compiler_params=_cparams("arbitrary"),
        name="na_bias",
    )(rpb.reshape(-1))


def _half_mask(e):
    lane = lax.broadcasted_iota(jnp.int32, (1, 128), 1)
    return (lane >= HEAD_DIM) if e else (lane < HEAD_DIM)


def _na_ctx_kernel(qkv_ref, *refs):
    o_ref, ko_ref, vo_ref = refs[-3:]
    scale = HEAD_DIM ** -0.5
    for hp in range(HEAD_PAIRS):
        q2 = qkv_ref[:, hp * 128:(hp + 1) * 128] * scale
        k2 = qkv_ref[:, D + hp * 128:D + (hp + 1) * 128]
        v2 = qkv_ref[:, 2 * D + hp * 128:2 * D + (hp + 1) * 128]
        for e in range(2):
            ko_ref[2 * hp + e] = k2[:, e * HEAD_DIM:(e + 1) * HEAD_DIM]
            vo_ref[2 * hp + e] = v2[:, e * HEAD_DIM:(e + 1) * HEAD_DIM]
        kb = k2.astype(BF16)
        vb = v2.astype(BF16)
        outs = []
        for e in range(2):
            qm = jnp.where(_half_mask(e), q2, 0.0).astype(BF16)
            s = _dot_nt(qm, kb)
            p = jnp.exp(s - jnp.max(s, axis=-1, keepdims=True))
            l = jnp.sum(p, axis=-1, keepdims=True)
            outs.append(_dot(p.astype(BF16), vb) / l)
        o_ref[:, hp * 128:(hp + 1) * 128] = jnp.where(_half_mask(0), outs[0], outs[1]).astype(BF16)


def _na_ctx_call(qkv, kv_prev, layer):
    kv_shape = jax.ShapeDtypeStruct((N_CTX, (DEPTH + 1) // 2, HEADS, L_CTX, HEAD_DIM), F32)
    kv_spec = pl.BlockSpec((None, None, HEADS, L_CTX, HEAD_DIM), lambda n: (n, layer, 0, 0, 0))
    in_specs = [pl.BlockSpec((L_CTX, 3 * D), lambda n: (n, 0))]
    args = [qkv]
    aliases = {}
    if kv_prev is not None:
        in_specs += [pl.BlockSpec(memory_space=pl.ANY)] * 2
        args += list(kv_prev)
        aliases = {1: 1, 2: 2}
    return pl.pallas_call(
        _na_ctx_kernel,
        out_shape=(jax.ShapeDtypeStruct((T_CTX, D), BF16), kv_shape, kv_shape),
        grid=(N_CTX,),
        in_specs=in_specs,
        out_specs=(pl.BlockSpec((L_CTX, D), lambda n: (n, 0)), kv_spec, kv_spec),
        input_output_aliases=aliases,
        compiler_params=_cparams("arbitrary"),
        name="na_ctx",
    )(*args)


def _window_of_row(r):
    r0 = min(max(r - WIN_H // 2, 0), GRID_H - WIN_H)
    r0a = r0 & ~1
    rows = WIN_H if r0a == r0 else WIN_H + 2
    return r0, r0a, rows


def _na_bands():
    bands = []
    for rs in (range(0, GRID_H // 2), range(GRID_H // 2, GRID_H)):
        lo = min(_window_of_row(r)[1] for r in rs)
        hi = max(_window_of_row(r)[1] + _window_of_row(r)[2] for r in rs)
        bands.append((slice(rs[0] * GRID_W, (rs[-1] + 1) * GRID_W), slice(lo * GRID_W, hi * GRID_W)))
    return tuple(bands)


_NA_BANDS = _na_bands()


def _na_lat_kernel(q_ref, k_ref, v_ref, kc_ref, vc_ref, te_ref, to_ref, o_ref, *scratch):
    @pl.when((pl.program_id(0) == 0) & (pl.program_id(1) == 0))
    def _():
        scratch[2][...] = jnp.zeros_like(scratch[2])

    q2 = q_ref[...] * (HEAD_DIM ** -0.5)
    kb = k_ref[...].astype(BF16)
    vb = v_ref[...].astype(BF16)
    kcb = jnp.concatenate([kc_ref[0], kc_ref[1]], axis=-1).astype(BF16)
    vcb = jnp.concatenate([vc_ref[0], vc_ref[1]], axis=-1).astype(BF16)
    outs = []
    for e in range(2):
        s_scr, sc_scr, p_scr, pc_scr, li_scr = (ref.at[e] for ref in scratch)
        qm = jnp.where(_half_mask(e), q2, 0.0).astype(BF16)
        for q_rows, k_rows in _NA_BANDS:
            s_scr[q_rows, k_rows] = _dot_nt(qm[q_rows], kb[k_rows])
        sc_scr[...] = _dot_nt(qm, kcb)
        for r in range(GRID_H):
            r0, r0a, nrows = _window_of_row(r)
            width = nrows * GRID_W
            blk = r0a - r + WIN_H
            t_ref, t_off = (te_ref, blk * GRID_W) if r % 2 == 0 else (to_ref, (blk + 1) * GRID_W)
            rows = slice(r * GRID_W, (r + 1) * GRID_W)
            cols = slice(r0a * GRID_W, r0a * GRID_W + width)
            sl = s_scr[rows, cols] + t_ref[e, :, t_off:t_off + width]
            if nrows != WIN_H:
                lane = lax.broadcasted_iota(jnp.int32, (1, width), 1)
                sl = jnp.where((lane < GRID_W) | (lane >= width - GRID_W), NEG_INF, sl)
            sc = sc_scr[rows, :]
            m = jnp.maximum(jnp.max(sl, axis=-1, keepdims=True), jnp.max(sc, axis=-1, keepdims=True))
            p_loc = jnp.exp(sl - m)
            p_ctx = jnp.exp(sc - m)
            l = jnp.sum(p_loc, axis=-1, keepdims=True) + jnp.sum(p_ctx, axis=-1, keepdims=True)
            p_scr[rows, cols] = p_loc.astype(BF16)
            pc_scr[rows, :] = p_ctx.astype(BF16)
            li_scr[rows, :] = jnp.broadcast_to(1.0 / l, (GRID_W, 128))
        o_loc = jnp.concatenate([_dot(p_scr[q_rows, k_rows], vb[k_rows]) for q_rows, k_rows in _NA_BANDS], axis=0)
        outs.append((o_loc + _dot(pc_scr[...], vcb)) * li_scr[...])
    o_ref[...] = jnp.where(_half_mask(0), outs[0], outs[1]).astype(BF16)


def _na_lat_call(qkv, cache_k, cache_v, layer, t_even, t_odd):
    row0 = T_CTX // L_LAT
    ctx_spec = pl.BlockSpec((None, None, 2, PAST, HEAD_DIM), lambda b, hp: (b, layer, hp, 0, 0))
    t_spec = pl.BlockSpec((2, GRID_W, BIAS_BLOCKS * GRID_W), lambda b, hp: (hp, 0, 0))
    return pl.pallas_call(
        _na_lat_kernel,
        out_shape=jax.ShapeDtypeStruct((T_LAT, D), BF16),
        grid=(N_LAT, HEAD_PAIRS),
        in_specs=[
            pl.BlockSpec((L_LAT, 128), lambda b, hp: (row0 + b, hp)),
            pl.BlockSpec((L_LAT, 128), lambda b, hp: (row0 + b, HEAD_PAIRS + hp)),
            pl.BlockSpec((L_LAT, 128), lambda b, hp: (row0 + b, 2 * HEAD_PAIRS + hp)),
            ctx_spec, ctx_spec, t_spec, t_spec,
        ],
        out_specs=pl.BlockSpec((L_LAT, 128), lambda b, hp: (b, hp)),
        scratch_shapes=[
            pltpu.VMEM((2, L_LAT, L_LAT), F32),
            pltpu.VMEM((2, L_LAT, PAST), F32),
            pltpu.VMEM((2, L_LAT, L_LAT), BF16),
            pltpu.VMEM((2, L_LAT, PAST), BF16),
            pltpu.VMEM((2, L_LAT, 128), F32),
        ],
        compiler_params=_cparams("arbitrary", "arbitrary"),
        name="na_lat",
    )(qkv, qkv, qkv, cache_k, cache_v, t_even, t_odd)


def _ssd_kernel(*refs, seq, gs, has_init, has_final, has_prev):
    it = iter(refs)
    x_ref, b_ref, c_ref, dt_ref, bias_ref, nega_ref, d_ref = (next(it) for _ in range(7))
    init_ref = next(it) if has_init else None
    if has_prev:
        next(it)
    y_ref = next(it)
    fin_ref = next(it) if has_final else None
    sloc_scr, sin_scr = next(it), next(it)

    nc = seq // CHUNK
    g0 = pl.program_id(1) * gs
    rows = SSD_R * SSD_P

    src = lax.broadcasted_iota(jnp.int32, (128, 128), 0)
    dst = lax.broadcasted_iota(jnp.int32, (128, 128), 1)
    want = ((dst >> 2) & 1) * SSD_HEADS + (g0 + (dst >> 3)) * SSD_R + (dst & 3)
    sel = jnp.where((dst < 8 * gs) & (src == want), 1.0, 0.0).astype(BF16)
    dt_raw = sum(_dot(p, sel) for p in _split3(dt_ref[...]))
    u = dt_raw + bias_ref[...]
    dt = jnp.maximum(u, 0.0) + jnp.log1p(jnp.exp(-jnp.abs(u)))
    a = dt * nega_ref[...]

    li = lax.broadcasted_iota(jnp.int32, (CHUNK, CHUNK), 0)
    si = lax.broadcasted_iota(jnp.int32, (CHUNK, CHUNK), 1)
    tril = jnp.where(si <= li, 1.0, 0.0).astype(BF16)
    triu = jnp.where(si >= li, 1.0, 0.0).astype(BF16)
    pre, suf = [], []
    for c in range(nc):
        parts = _split3(a[c * CHUNK:(c + 1) * CHUNK])
        pre.append(sum(_dot(tril, p) for p in parts))
        suf.append(sum(_dot(triu, p) for p in parts))
    pre = jnp.concatenate(pre, axis=0)
    suf = jnp.concatenate(suf, axis=0)
    fwd_lane = (lax.broadcasted_iota(jnp.int32, (1, 128), 1) & SSD_R) == 0
    e1 = jnp.where(fwd_lane, pre, suf)
    e2 = jnp.where(fwd_lane, suf, pre) - a
    e1s = e1 * LOG2E
    e1s_t = e1s.T
    dt_t = dt.T
    ysc_t = jnp.exp(e1).T
    w2_t = jnp.exp(e2).T

    def expand(t, off, cols):
        return jnp.concatenate(
            [jnp.broadcast_to(t[off + h:off + h + 1, cols], (SSD_P, cols.stop - cols.start))
             for h in range(SSD_R)], axis=0)

    def chunk_decay(off, col):
        return expand(ysc_t, off, slice(col, col + 1))

    x_t = x_ref[...].T

    for gi in range(gs):
        fwd, bwd = 8 * gi, 8 * gi + SSD_R
        xg_t = x_t[gi * rows:(gi + 1) * rows]
        b_cols = slice(gi * SSD_N, (gi + 1) * SSD_N)
        y_cols = slice(gi * rows, (gi + 1) * rows)

        for c in range(nc):
            cols = slice(c * CHUNK, (c + 1) * CHUNK)
            xc = xg_t[:, cols]
            xw = jnp.concatenate([xc * (expand(dt_t, fwd, cols) * expand(w2_t, fwd, cols)),
                                  xc * (expand(dt_t, bwd, cols) * expand(w2_t, bwd, cols))], axis=0)
            sloc_scr[gi, c] = _dot(xw.astype(BF16), b_ref[cols, b_cols].astype(BF16))

        heads = slice(gi * SSD_R, (gi + 1) * SSD_R)
        s_f = init_ref[0, heads].reshape(rows, SSD_N) if has_init else None
        for c in range(nc):
            if s_f is not None:
                sin_scr[gi, c, 0:rows] = s_f
                s_f = s_f * chunk_decay(fwd, c * CHUNK + CHUNK - 1) + sloc_scr[gi, c, 0:rows]
            else:
                s_f = sloc_scr[gi, c, 0:rows]
        s_b = init_ref[1, heads].reshape(rows, SSD_N) if has_init else None
        for c in reversed(range(nc)):
            if s_b is not None:
                sin_scr[gi, c, rows:2 * rows] = s_b
                s_b = s_b * chunk_decay(bwd, c * CHUNK) + sloc_scr[gi, c, rows:2 * rows]
            else:
                s_b = sloc_scr[gi, c, rows:2 * rows]
        if has_final:
            fin_ref[0, heads] = s_f.reshape(SSD_R, SSD_P, SSD_N)
            fin_ref[1, heads] = s_b.reshape(SSD_R, SSD_P, SSD_N)

        for c in range(nc):
            cols = slice(c * CHUNK, (c + 1) * CHUNK)
            bc = b_ref[cols, b_cols].astype(BF16)
            cc = c_ref[cols, b_cols].astype(BF16)
            cb_t = _dot_nt(bc, cc)
            xc = xg_t[:, cols]
            xdf = xc * expand(dt_t, fwd, cols)
            xdb = xc * expand(dt_t, bwd, cols)
            entering = []
            if has_init or c > 0:
                entering.append((slice(0, rows), fwd))
            if has_init or c < nc - 1:
                entering.append((slice(rows, 2 * rows), bwd))
            off = None
            if entering:
                s_in = jnp.concatenate([sin_scr[gi, c, r] for r, _ in entering], axis=0)
                prod = _dot_nt(s_in.astype(BF16), cc)
                for k, (_, base) in enumerate(entering):
                    term = prod[k * rows:(k + 1) * rows] * expand(ysc_t, base, cols)
                    off = term if off is None else off + term
            y_heads = []
            for h in range(SSD_R):
                pf_col = e1s[cols, fwd + h:fwd + h + 1]
                rb_col = e1s[cols, bwd + h:bwd + h + 1]
                pf_row = e1s_t[fwd + h:fwd + h + 1, cols]
                rb_row = e1s_t[bwd + h:bwd + h + 1, cols]
                m_f = jnp.exp2(jnp.where(li <= si, pf_row - pf_col, NEG_INF)) * cb_t
                m_b = jnp.exp2(jnp.where(li >= si, rb_row - rb_col, NEG_INF)) * cb_t
                hs = slice(h * SSD_P, (h + 1) * SSD_P)
                lhs = jnp.concatenate([xdf[hs], xdb[hs]], axis=1).astype(BF16)
                rhs = jnp.concatenate([m_f, m_b], axis=0).astype(BF16)
                y_h = _dot(lhs, rhs)
                y_heads.append(y_h if off is None else y_h + off[hs])
            y_t = jnp.concatenate(y_heads, axis=0)
            y_ref[cols, y_cols] = y_t.T + x_ref[cols, y_cols] * d_ref[:, y_cols]


def _ssd_call(xbc, dt, dt_bias, a_log, d, state, fin_prev, layer, *, latent):
    seq = L_LAT if latent else L_CTX
    n_seq = N_LAT if latent else N_CTX
    gs = GS_LAT if latent else GS_CTX
    row0 = T_CTX // seq if latent else 0
    nc = seq // CHUNK
    xw, bw = gs * SSD_R * SSD_P, gs * SSD_N
    xcol0 = 0
    bcol0 = D_INNER // bw
    ccol0 = bcol0 + SSD_GROUPS // gs
    step_spec = lambda w: pl.BlockSpec((None, 1, w), lambda n, k: (k, 0, 0))
    in_specs = [
        pl.BlockSpec((seq, xw), lambda n, k: (row0 + n, xcol0 + k)),
        pl.BlockSpec((seq, bw), lambda n, k: (row0 + n, bcol0 + k)),
        pl.BlockSpec((seq, bw), lambda n, k: (row0 + n, ccol0 + k)),
        pl.BlockSpec((seq, 128), lambda n, k: (row0 + n, 0)),
        step_spec(128), step_spec(128), step_spec(xw),
    ]
    args = [xbc, xbc, xbc, dt, _step_lanes(dt_bias, gs), _step_lanes(-jnp.exp(a_log), gs),
            jnp.repeat(d, SSD_P).reshape(SSD_GROUPS // gs, 1, xw)]
    state_block = (None, None, 2, gs * SSD_R, SSD_P, SSD_N)
    aliases = {}
    if latent:
        in_specs.append(pl.BlockSpec(state_block, lambda n, k: (n, layer, 0, k, 0, 0)))
        args.append(state)
    elif fin_prev is not None:
        in_specs.append(pl.BlockSpec(memory_space=pl.ANY))
        args.append(fin_prev)
        aliases = {len(args) - 1: 1}
    y_shape = jax.ShapeDtypeStruct((n_seq * seq, D_INNER), F32)
    y_spec = pl.BlockSpec((seq, xw), lambda n, k: (n, k))
    if latent:
        out_shape, out_specs = y_shape, y_spec
    else:
        out_shape = (y_shape, jax.ShapeDtypeStruct((N_CTX, DEPTH // 2, 2, SSD_HEADS, SSD_P, SSD_N), F32))
        out_specs = (y_spec, pl.BlockSpec(state_block, lambda n, k: (n, layer, 0, k, 0, 0)))
    scratch = pltpu.VMEM((gs, nc, 2 * SSD_R * SSD_P, SSD_N), F32)
    return pl.pallas_call(
        functools.partial(_ssd_kernel, seq=seq, gs=gs, has_init=latent, has_final=not latent,
                          has_prev=bool(aliases)),
        out_shape=out_shape,
        grid=(n_seq, SSD_GROUPS // gs),
        in_specs=in_specs,
        out_specs=out_specs,
        input_output_aliases=aliases,
        scratch_shapes=[scratch, scratch],
        compiler_params=_cparams("arbitrary", "arbitrary"),
        name="ssd_scan_lat" if latent else "ssd_scan_ctx",
    )(*args)


def _step_lanes(p, gs):
    q = p.reshape(2, SSD_GROUPS, SSD_R).transpose(1, 0, 2).reshape(SSD_GROUPS // gs, 8 * gs)
    return jnp.pad(q, ((0, 0), (0, 128 - 8 * gs))).reshape(SSD_GROUPS // gs, 1, 128)


def kernel(x_prompt, x_sample, cache_k, cache_v, state_ssm, c, c_ctx, ada_w, ada_b, norm_mix_g, norm_ffn_g,
           ffn_w_gate_up, ffn_w_down, na_w_qkv, na_w_o, na_rpb, ssd_w_in, ssd_conv_w, ssd_conv_b,
           ssd_dt_bias, ssd_a_log, ssd_d, ssd_norm_g, ssd_w_out, final_norm_g):
    cond = jnp.zeros((N_COND, D), F32).at[0].set(c_ctx).at[1:1 + N_LAT].set(c)
    mods = _ada_call(cond, ada_w, ada_b).reshape(DEPTH, N_COND, 6, 1, D)
    x = (x_prompt.reshape(T_CTX, D), x_sample.reshape(T_LAT, D))
    final_g = final_norm_g.reshape(1, D)
    w_qkv, w_o = na_w_qkv.astype(BF16), na_w_o.astype(BF16)
    w_in, w_out = ssd_w_in.astype(BF16), ssd_w_out.astype(BF16)
    w_gu, w_d = ffn_w_gate_up.astype(BF16), ffn_w_down.astype(BF16)
    conv_b = ssd_conv_b.reshape(DEPTH // 2, 1, CONV_DIM)

    new_kv, new_s = None, None
    for i in range(DEPTH):
        j = i // 2
        m = [mods[i, :, t] for t in range(6)]
        g_mix = norm_mix_g[i].reshape(1, D)
        if i % 2 == 0:
            qkv = _qkv_call(x, g_mix, m[0], m[1], w_qkv, j)
            o_ctx, *new_kv = _na_ctx_call(qkv, new_kv, j)
            t_even, t_odd = _bias_call(na_rpb[j])
            o_lat = _na_lat_call(qkv, cache_k, cache_v, j, t_even, t_odd)
            x = _oproj_call(o_ctx, o_lat, x, m[2], w_o, j)
        else:
            w_dt = jnp.pad(w_in[j, :, ZX_DIM:], ((0, 0), (0, 128 - 2 * SSD_HEADS)))
            z, xbc, dt = _inproj_call(x, g_mix, m[0], m[1], w_in, w_dt, ssd_conv_w, conv_b, j)
            scan_args = (xbc, dt, ssd_dt_bias[j], ssd_a_log[j], ssd_d[j])
            y_ctx, new_s = _ssd_call(*scan_args, None, new_s, j, latent=False)
            y_lat = _ssd_call(*scan_args, state_ssm, None, j, latent=True)
            x = _ssd_out_call(y_ctx, y_lat, z, x, m[2], ssd_norm_g[j].reshape(1, D_INNER), w_out, j)
        ffn_args = (x, norm_ffn_g[i].reshape(1, D), m[3], m[4], m[5], w_gu, w_d, final_g, i)
        if i < DEPTH - 1:
            x = _ffn_call(*ffn_args, final_norm=False)

    ctx_tiles = T_CTX // TM_FFN
    y_prompt = _ffn_call(*ffn_args, final_norm=True, tile0=0, n_tiles=ctx_tiles)
    y_sample = _ffn_call(*ffn_args, final_norm=True, tile0=ctx_tiles, n_tiles=T_LAT // TM_FFN)
    return (y_prompt.reshape(N_CTX, L_CTX, D), y_sample.reshape(N_LAT, L_LAT, D), new_kv[0], new_kv[1], new_s)
```

```python
import functools

import jax
import jax.numpy as jnp
from jax import lax
from jax.experimental import pallas as pl
from jax.experimental.pallas import tpu as pltpu

F32 = jnp.float32
BF16 = jnp.bfloat16

D = 1024
N_CTX, L_CTX = 32, 256
N_LAT, L_LAT = 4, 1024
T_CTX = N_CTX * L_CTX
T_LAT = N_LAT * L_LAT
T_ALL = T_CTX + T_LAT
DEPTH = 4
N_COND = 8
PAST = 512
GRID_W = 64
GRID_H = L_LAT // GRID_W
HEADS = 16
HEAD_DIM = 64
HEAD_PAIRS = HEADS // 2
WIN_H, WIN_W = 8, 16
RPB_H, RPB_W = 2 * WIN_H - 1, 2 * WIN_W - 1
D_INNER = 2048
SSD_HEADS = 32
SSD_P = 64
SSD_GROUPS = 8
SSD_R = SSD_HEADS // SSD_GROUPS
SSD_N = 128
CHUNK = 128
CONV_DIM = D_INNER + 2 * SSD_GROUPS * SSD_N
ZX_DIM = D_INNER + CONV_DIM
D_FF = 2816
EPS = 1e-6
NEG_INF = float("-inf")
LOG2E = 1.4426950408889634

TM = 1024
TM_FFN = 1024
MXU_DIM = 256
FF_CUTS = (0, (D_FF // MXU_DIM + 1) // 2 * MXU_DIM, D_FF)
TN_QKV = 1024
TN_IN = 1024
MM_COLS = 512
TM_OUT = 512
GS_CTX = 8
GS_LAT = 4
BIAS_BLOCKS = 18
VMEM_LIMIT = 56 * 1024 * 1024


def _cparams(*sem):
    return pltpu.CompilerParams(dimension_semantics=sem, vmem_limit_bytes=VMEM_LIMIT)


def _cond_of_tile(i, tm):
    return jnp.maximum((i * tm) // L_LAT - (T_CTX // L_LAT - 1), 0)


def _silu(x):
    return x * (0.5 * jnp.tanh(0.5 * x) + 0.5)


def _rms(x, g):
    return x * lax.rsqrt(jnp.mean(x * x, axis=-1, keepdims=True) + EPS) * g


def _norm_mod(x, g, shift, scale):
    return _rms(x, g) * (1.0 + scale) + shift


def _dot(a, b):
    return jnp.dot(a, b, preferred_element_type=F32)


def _dot_nt(a, b):
    return lax.dot_general(a, b, (((1,), (1,)), ((), ())), preferred_element_type=F32)


def _split3(x):
    hi = x.astype(BF16)
    r = x - hi.astype(F32)
    mid = r.astype(BF16)
    lo = (r - mid.astype(F32)).astype(BF16)
    return hi, mid, lo


def _ada_kernel(c_ref, w_ref, b_ref, o_ref):
    s = _silu(c_ref[...]).astype(BF16)
    o_ref[...] = _dot(s, w_ref[...].astype(BF16)) + b_ref[...]


def _ada_call(cond, ada_w, ada_b):
    tn = 1536
    return pl.pallas_call(
        _ada_kernel,
        out_shape=jax.ShapeDtypeStruct((DEPTH, N_COND, 6 * D), F32),
        grid=(DEPTH, 6 * D // tn),
        in_specs=[
            pl.BlockSpec((N_COND, D), lambda l, j: (0, 0)),
            pl.BlockSpec((None, D, tn), lambda l, j: (l, 0, j)),
            pl.BlockSpec((None, 1, tn), lambda l, j: (l, 0, j)),
        ],
        out_specs=pl.BlockSpec((None, N_COND, tn), lambda l, j: (l, 0, j)),
        compiler_params=_cparams("arbitrary", "arbitrary"),
        name="ada",
    )(cond, ada_w, ada_b.reshape(DEPTH, 1, 6 * D))


def _mod_spec(tm, tile0=0):
    return pl.BlockSpec((None, 1, D), lambda i, *_: (_cond_of_tile(i + tile0, tm), 0, 0))


def _stream_specs(streams, tm, width):
    n_ctx = T_CTX // tm
    lat0 = 0 if isinstance(streams, tuple) else n_ctx
    arrays = streams if isinstance(streams, tuple) else (streams, streams)
    specs = [pl.BlockSpec((tm, width), lambda i, *_: (jnp.minimum(i, n_ctx - 1), 0)),
             pl.BlockSpec((tm, width), lambda i, *_: (lat0 + jnp.maximum(i - n_ctx, 0), 0))]
    return list(arrays), specs


def _on_stream(tm, fn, ctx_ref, lat_ref, extra=True):
    is_ctx = pl.program_id(0) < T_CTX // tm
    pl.when(is_ctx & extra)(lambda: fn(ctx_ref))
    pl.when(jnp.logical_not(is_ctx) & extra)(lambda: fn(lat_ref))


def _qkv_kernel(xc_ref, xl_ref, g_ref, sh_ref, sc_ref, w_ref, o_ref, h_scr):
    def prologue(x_ref):
        h_scr[...] = _norm_mod(x_ref[...], g_ref[...], sh_ref[...], sc_ref[...]).astype(BF16)

    _on_stream(TM, prologue, xc_ref, xl_ref, pl.program_id(1) == 0)
    o_ref[...] = _dot(h_scr[...], w_ref[...])


def _qkv_call(x, g, shift, scale, w_all, layer):
    n = w_all.shape[2]
    x_args, x_specs = _stream_specs(x, TM, D)
    return pl.pallas_call(
        _qkv_kernel,
        out_shape=jax.ShapeDtypeStruct((T_ALL, n), F32),
        grid=(T_ALL // TM, n // TN_QKV),
        in_specs=x_specs + [
            pl.BlockSpec((1, D), lambda i, j: (0, 0)),
            _mod_spec(TM),
            _mod_spec(TM),
            pl.BlockSpec((None, D, TN_QKV), lambda i, j: (layer, 0, j)),
        ],
        out_specs=pl.BlockSpec((TM, TN_QKV), lambda i, j: (i, j)),
        scratch_shapes=[pltpu.VMEM((TM, D), BF16)],
        compiler_params=_cparams("arbitrary", "arbitrary"),
        name="na_qkv",
    )(*x_args, g, shift, scale, w_all)


def _inproj_kernel(x_ref, g_ref, sh_ref, sc_ref, w_ref, wdt_ref, cw_ref, cb_ref, z_ref, xbc_ref, dt_ref,
                   h_scr, c_scr):
    i = pl.program_id(0)
    s = pl.program_id(1)
    n_z = D_INNER // TN_IN

    @pl.when((i == 0) & (s == 0))
    def _():
        c_scr[0:8, :] = jnp.zeros((8, TN_IN), F32)
        c_scr[TM + 8:TM + 16, :] = jnp.zeros((8, TN_IN), F32)

    @pl.when(s == 0)
    def _():
        h = _norm_mod(x_ref[...], g_ref[...], sh_ref[...], sc_ref[...]).astype(BF16)
        h_scr[...] = h
        dt_ref[...] = _dot(h, wdt_ref[...])

    @pl.when(s < n_z)
    def _():
        z_ref[...] = _dot(h_scr[...], w_ref[...])

    @pl.when(s >= n_z)
    def _():
        seq = jnp.where(i < T_CTX // TM, L_CTX, L_LAT)
        pos = lax.broadcasted_iota(jnp.int32, (TM, 1), 0) & (seq - 1)
        for m0 in range(0, TN_IN, MM_COLS):
            cols = slice(m0, m0 + MM_COLS)
            acc = _dot(h_scr[...], w_ref[:, cols])
            c_scr[8:TM + 8, cols] = acc
            prev = jnp.where(pos == 0, 0.0, c_scr[7:TM + 7, cols])
            nxt = jnp.where(pos == seq - 1, 0.0, c_scr[9:TM + 9, cols])
            y = cw_ref[0:1, cols] * prev + cw_ref[1:2, cols] * acc + cw_ref[2:3, cols] * nxt + cb_ref[:, cols]
            xbc_ref[:, cols] = _silu(y)


def _inproj_call(x, g, shift, scale, w_all, w_dt, conv_w_all, conv_b_all, layer):
    n_z = D_INNER // TN_IN
    n_steps = ZX_DIM // TN_IN
    conv_tile = lambda s: jnp.maximum(s - n_z, 0)
    return pl.pallas_call(
        _inproj_kernel,
        out_shape=(jax.ShapeDtypeStruct((T_ALL, D_INNER), F32),
                   jax.ShapeDtypeStruct((T_ALL, CONV_DIM), F32),
                   jax.ShapeDtypeStruct((T_ALL, 128), F32)),
        grid=(T_ALL // TM, n_steps),
        in_specs=[
            pl.BlockSpec((TM, D), lambda i, s: (i, 0)),
            pl.BlockSpec((1, D), lambda i, s: (0, 0)),
            _mod_spec(TM),
            _mod_spec(TM),
            pl.BlockSpec((None, D, TN_IN), lambda i, s: (layer, 0, s)),
            pl.BlockSpec((D, 128), lambda i, s: (0, 0)),
            pl.BlockSpec((None, 3, TN_IN), lambda i, s: (layer, 0, conv_tile(s))),
            pl.BlockSpec((None, 1, TN_IN), lambda i, s: (layer, 0, conv_tile(s))),
        ],
        out_specs=(pl.BlockSpec((TM, TN_IN), lambda i, s: (i, jnp.minimum(s, n_z - 1))),
                   pl.BlockSpec((TM, TN_IN), lambda i, s: (i, conv_tile(s))),
                   pl.BlockSpec((TM, 128), lambda i, s: (i, 0))),
        scratch_shapes=[pltpu.VMEM((TM, D), BF16), pltpu.VMEM((TM + 16, TN_IN), F32)],
        compiler_params=_cparams("arbitrary", "arbitrary"),
        name="ssd_inproj",
    )(x, g, shift, scale, w_all, w_dt, conv_w_all, conv_b_all)


def _oproj_kernel(ac_ref, al_ref, xc_ref, xl_ref, gt_ref, w_ref, o_ref):
    def run(refs):
        a_ref, x_ref = refs
        o_ref[...] = x_ref[...] + gt_ref[...] * _dot(a_ref[...], w_ref[...])

    _on_stream(TM_OUT, run, (ac_ref, xc_ref), (al_ref, xl_ref))


def _oproj_call(a_ctx, a_lat, x, gate, w_all, layer):
    a_args, a_specs = _stream_specs((a_ctx, a_lat), TM_OUT, D)
    x_args, x_specs = _stream_specs(x, TM_OUT, D)
    return pl.pallas_call(
        _oproj_kernel,
        out_shape=jax.ShapeDtypeStruct((T_ALL, D), F32),
        grid=(T_ALL // TM_OUT,),
        in_specs=a_specs + x_specs + [
            _mod_spec(TM_OUT),
            pl.BlockSpec((None, D, D), lambda i: (layer, 0, 0)),
        ],
        out_specs=pl.BlockSpec((TM_OUT, D), lambda i: (i, 0)),
        compiler_params=_cparams("arbitrary"),
        name="na_oproj",
    )(*a_args, *x_args, gate, w_all)


def _ssd_out_kernel(yc_ref, yl_ref, z_ref, x_ref, gt_ref, ng_ref, w_ref, o_ref):
    def run(y_ref):
        v = y_ref[...] * _silu(z_ref[...])
        v = _rms(v, ng_ref[...]).astype(BF16)
        o_ref[...] = x_ref[...] + gt_ref[...] * _dot(v, w_ref[...])

    _on_stream(TM_OUT, run, yc_ref, yl_ref)


def _ssd_out_call(y_ctx, y_lat, z, x, gate, norm_g, w_all, layer):
    y_args, y_specs = _stream_specs((y_ctx, y_lat), TM_OUT, D_INNER)
    return pl.pallas_call(
        _ssd_out_kernel,
        out_shape=jax.ShapeDtypeStruct((T_ALL, D), F32),
        grid=(T_ALL // TM_OUT,),
        in_specs=y_specs + [
            pl.BlockSpec((TM_OUT, D_INNER), lambda i: (i, 0)),
            pl.BlockSpec((TM_OUT, D), lambda i: (i, 0)),
            _mod_spec(TM_OUT),
            pl.BlockSpec((1, D_INNER), lambda i: (0, 0)),
            pl.BlockSpec((None, D_INNER, D), lambda i: (layer, 0, 0)),
        ],
        out_specs=pl.BlockSpec((TM_OUT, D), lambda i: (i, 0)),
        compiler_params=_cparams("arbitrary"),
        name="ssd_out",
    )(*y_args, z, x, gate, norm_g, w_all)


def _ffn_kernel(x_ref, g_ref, sh_ref, sc_ref, gt_ref, wgu_ref, wd_ref, fg_ref, o_ref, *, final_norm):
    half = TM_FFN // 2
    for r in range(2):
        rows = slice(r * half, (r + 1) * half)
        h = _norm_mod(x_ref[rows, :], g_ref[...], sh_ref[...], sc_ref[...]).astype(BF16)
        acc = None
        for lo, hi in zip(FF_CUTS[:-1], FF_CUTS[1:]):
            gate_cols = slice(lo, hi)
            up_cols = slice(D_FF + lo, D_FF + hi)
            act = (_silu(_dot(h, wgu_ref[:, gate_cols])) * _dot(h, wgu_ref[:, up_cols])).astype(BF16)
            part = _dot(act, wd_ref[gate_cols, :])
            acc = part if acc is None else acc + part
        y = x_ref[rows, :] + gt_ref[...] * acc
        if final_norm:
            y = _rms(y, fg_ref[...])
        o_ref[rows, :] = y


def _ffn_call(x, g, shift, scale, gate, w_gu_all, w_d_all, final_g, layer, final_norm,
              tile0=0, n_tiles=T_ALL // TM_FFN):
    resident = dict(pipeline_mode=pl.Buffered(1))
    return pl.pallas_call(
        functools.partial(_ffn_kernel, final_norm=final_norm),
        out_shape=jax.ShapeDtypeStruct((n_tiles * TM_FFN, D), F32),
        grid=(n_tiles,),
        in_specs=[
            pl.BlockSpec((TM_FFN, D), lambda i: (i + tile0, 0)),
            pl.BlockSpec((1, D), lambda i: (0, 0)),
            _mod_spec(TM_FFN, tile0),
            _mod_spec(TM_FFN, tile0),
            _mod_spec(TM_FFN, tile0),
            pl.BlockSpec((None, D, 2 * D_FF), lambda i: (layer, 0, 0), **resident),
            pl.BlockSpec((None, D_FF, D), lambda i: (layer, 0, 0), **resident),
            pl.BlockSpec((1, D), lambda i: (0, 0)),
        ],
        out_specs=pl.BlockSpec((TM_FFN, D), lambda i: (i, 0)),
        compiler_params=_cparams("arbitrary"),
        name="ffn",
    )(x, g, shift, scale, gate, w_gu_all, w_d_all, final_g)


def _bias_kernel(rpb_ref, te_ref, to_ref):
    h = pl.program_id(0)
    qc = lax.broadcasted_iota(jnp.int32, (GRID_W, 128), 0)
    lane = lax.broadcasted_iota(jnp.int32, (GRID_W, 128), 1)
    kc = lane & (GRID_W - 1)
    upper = lane >= GRID_W
    idx = jnp.clip(kc - qc + (WIN_W - 1), 0, RPB_W - 1)
    start = jnp.clip(qc - WIN_W // 2, 0, GRID_W - WIN_W)
    in_win = (kc >= start) & (kc < start + WIN_W)

    def pair_tile(dr_lo, dr_hi):
        def ok(dr):
            return 0 <= dr < RPB_H

        acc = jnp.zeros((GRID_W, 128), F32)
        for d in range(RPB_W):
            lo = rpb_ref[h * (RPB_H * RPB_W) + dr_lo * RPB_W + d] if ok(dr_lo) else 0.0
            hi = rpb_ref[h * (RPB_H * RPB_W) + dr_hi * RPB_W + d] if ok(dr_hi) else 0.0
            acc = jnp.where(idx == d, jnp.where(upper, hi, lo), acc)
        if ok(dr_lo) and ok(dr_hi):
            valid = in_win
        elif ok(dr_lo):
            valid = in_win & jnp.logical_not(upper)
        elif ok(dr_hi):
            valid = in_win & upper
        else:
            return jnp.full((GRID_W, 128), NEG_INF, F32)
        return jnp.where(valid, acc, NEG_INF)

    for t in range(BIAS_BLOCKS // 2):
        te_ref[:, t * 128:(t + 1) * 128] = pair_tile(2 * t - 1, 2 * t)
        to_ref[:, t * 128:(t + 1) * 128] = pair_tile(2 * t - 2, 2 * t - 1)


def _bias_call(rpb):
    shape = jax.ShapeDtypeStruct((HEADS, GRID_W, BIAS_BLOCKS * GRID_W), F32)
    spec = pl.BlockSpec((None, GRID_W, BIAS_BLOCKS * GRID_W), lambda h: (h, 0, 0))
    return pl.pallas_call(
        _bias_kernel,
        out_shape=(shape, shape),
        grid=(HEADS,),
        in_specs=[pl.BlockSpec(memory_space=pltpu.SMEM)],
        out_specs=(spec, spec),
        compiler_params=_cparams("arbitrary"),
        name="na_bias",
    )(rpb.reshape(-1))


def _half_mask(e):
    lane = lax.broadcasted_iota(jnp.int32, (1, 128), 1)
    return (lane >= HEAD_DIM) if e else (lane < HEAD_DIM)


def _na_ctx_kernel(qkv_ref, *refs):
    o_ref, ko_ref, vo_ref = refs[-3:]
    scale = HEAD_DIM ** -0.5
    for hp in range(HEAD_PAIRS):
        q2 = qkv_ref[:, hp * 128:(hp + 1) * 128] * scale
        k2 = qkv_ref[:, D + hp * 128:D + (hp + 1) * 128]
        v2 = qkv_ref[:, 2 * D + hp * 128:2 * D + (hp + 1) * 128]
        for e in range(2):
            ko_ref[2 * hp + e] = k2[:, e * HEAD_DIM:(e + 1) * HEAD_DIM]
            vo_ref[2 * hp + e] = v2[:, e * HEAD_DIM:(e + 1) * HEAD_DIM]
        kb = k2.astype(BF16)
        vb = v2.astype(BF16)
        outs = []
        for e in range(2):
            qm = jnp.where(_half_mask(e), q2, 0.0).astype(BF16)
            s = _dot_nt(qm, kb)
            p = jnp.exp(s - jnp.max(s, axis=-1, keepdims=True))
            l = jnp.sum(p, axis=-1, keepdims=True)
            outs.append(_dot(p.astype(BF16), vb) / l)
        o_ref[:, hp * 128:(hp + 1) * 128] = jnp.where(_half_mask(0), outs[0], outs[1]).astype(BF16)


def _na_ctx_call(qkv, kv_prev, layer):
    kv_shape = jax.ShapeDtypeStruct((N_CTX, (DEPTH + 1) // 2, HEADS, L_CTX, HEAD_DIM), F32)
    kv_spec = pl.BlockSpec((None, None, HEADS, L_CTX, HEAD_DIM), lambda n: (n, layer, 0, 0, 0))
    in_specs = [pl.BlockSpec((L_CTX, 3 * D), lambda n: (n, 0))]
    args = [qkv]
    aliases = {}
    if kv_prev is not None:
        in_specs += [pl.BlockSpec(memory_space=pl.ANY)] * 2
        args += list(kv_prev)
        aliases = {1: 1, 2: 2}
    return pl.pallas_call(
        _na_ctx_kernel,
        out_shape=(jax.ShapeDtypeStruct((T_CTX, D), BF16), kv_shape, kv_shape),
        grid=(N_CTX,),
        in_specs=in_specs,
        out_specs=(pl.BlockSpec((L_CTX, D), lambda n: (n, 0)), kv_spec, kv_spec),
        input_output_aliases=aliases,
        compiler_params=_cparams("arbitrary"),
        name="na_ctx",
    )(*args)


def _window_of_row(r):
    r0 = min(max(r - WIN_H // 2, 0), GRID_H - WIN_H)
    r0a = r0 & ~1
    rows = WIN_H if r0a == r0 else WIN_H + 2
    return r0, r0a, rows


def _na_bands():
    bands = []
    for rs in (range(0, GRID_H // 2), range(GRID_H // 2, GRID_H)):
        lo = min(_window_of_row(r)[1] for r in rs)
        hi = max(_window_of_row(r)[1] + _window_of_row(r)[2] for r in rs)
        bands.append((slice(rs[0] * GRID_W, (rs[-1] + 1) * GRID_W), slice(lo * GRID_W, hi * GRID_W)))
    return tuple(bands)


_NA_BANDS = _na_bands()


def _na_lat_kernel(q_ref, k_ref, v_ref, kc_ref, vc_ref, te_ref, to_ref, o_ref, *scratch):
    @pl.when((pl.program_id(0) == 0) & (pl.program_id(1) == 0))
    def _():
        scratch[2][...] = jnp.zeros_like(scratch[2])

    q2 = q_ref[...] * (HEAD_DIM ** -0.5)
    kb = k_ref[...].astype(BF16)
    vb = v_ref[...].astype(BF16)
    kc_t = jnp.concatenate([kc_ref[0], kc_ref[1]], axis=0).astype(BF16)
    vc_t = jnp.concatenate([vc_ref[0], vc_ref[1]], axis=0).astype(BF16)
    outs = []
    for e in range(2):
        s_scr, sc_scr, p_scr, pc_scr, li_scr = (ref.at[e] for ref in scratch)
        qm = jnp.where(_half_mask(e), q2, 0.0).astype(BF16)
        for q_rows, k_rows in _NA_BANDS:
            s_scr[q_rows, k_rows] = _dot_nt(qm[q_rows], kb[k_rows])
        sc_scr[...] = _dot(qm, kc_t)
        for r in range(GRID_H):
            r0, r0a, nrows = _window_of_row(r)
            width = nrows * GRID_W
            blk = r0a - r + WIN_H
            t_ref, t_off = (te_ref, blk * GRID_W) if r % 2 == 0 else (to_ref, (blk + 1) * GRID_W)
            rows = slice(r * GRID_W, (r + 1) * GRID_W)
            cols = slice(r0a * GRID_W, r0a * GRID_W + width)
            sl = s_scr[rows, cols] + t_ref[e, :, t_off:t_off + width]
            if nrows != WIN_H:
                lane = lax.broadcasted_iota(jnp.int32, (1, width), 1)
                sl = jnp.where((lane < GRID_W) | (lane >= width - GRID_W), NEG_INF, sl)
            sc = sc_scr[rows, :]
            m = jnp.maximum(jnp.max(sl, axis=-1, keepdims=True), jnp.max(sc, axis=-1, keepdims=True))
            p_loc = jnp.exp(sl - m)
            p_ctx = jnp.exp(sc - m)
            l = jnp.sum(p_loc, axis=-1, keepdims=True) + jnp.sum(p_ctx, axis=-1, keepdims=True)
            p_scr[rows, cols] = p_loc.astype(BF16)
            pc_scr[rows, :] = p_ctx.astype(BF16)
            li_scr[rows, :] = jnp.broadcast_to(1.0 / l, (GRID_W, 128))
        o_loc = jnp.concatenate([_dot(p_scr[q_rows, k_rows], vb[k_rows]) for q_rows, k_rows in _NA_BANDS], axis=0)
        outs.append((o_loc + _dot_nt(pc_scr[...], vc_t)) * li_scr[...])
    o_ref[...] = jnp.where(_half_mask(0), outs[0], outs[1]).astype(BF16)


def _na_lat_call(qkv, cache_k, cache_v, layer, t_even, t_odd):
    row0 = T_CTX // L_LAT
    ctx_spec = pl.BlockSpec((None, None, 2, HEAD_DIM, PAST), lambda b, hp: (b, layer, hp, 0, 0))
    t_spec = pl.BlockSpec((2, GRID_W, BIAS_BLOCKS * GRID_W), lambda b, hp: (hp, 0, 0))
    return pl.pallas_call(
        _na_lat_kernel,
        out_shape=jax.ShapeDtypeStruct((T_LAT, D), BF16),
        grid=(N_LAT, HEAD_PAIRS),
        in_specs=[
            pl.BlockSpec((L_LAT, 128), lambda b, hp: (row0 + b, hp)),
            pl.BlockSpec((L_LAT, 128), lambda b, hp: (row0 + b, HEAD_PAIRS + hp)),
            pl.BlockSpec((L_LAT, 128), lambda b, hp: (row0 + b, 2 * HEAD_PAIRS + hp)),
            ctx_spec, ctx_spec, t_spec, t_spec,
        ],
        out_specs=pl.BlockSpec((L_LAT, 128), lambda b, hp: (b, hp)),
        scratch_shapes=[
            pltpu.VMEM((2, L_LAT, L_LAT), F32),
            pltpu.VMEM((2, L_LAT, PAST), F32),
            pltpu.VMEM((2, L_LAT, L_LAT), BF16),
            pltpu.VMEM((2, L_LAT, PAST), BF16),
            pltpu.VMEM((2, L_LAT, 128), F32),
        ],
        compiler_params=_cparams("arbitrary", "arbitrary"),
        name="na_lat",
    )(qkv, qkv, qkv, cache_k, cache_v, t_even, t_odd)


def _ssd_kernel(*refs, seq, gs, has_init, has_final, has_prev):
    it = iter(refs)
    x_ref, b_ref, c_ref, dt_ref, bias_ref, nega_ref, d_ref = (next(it) for _ in range(7))
    init_ref = next(it) if has_init else None
    if has_prev:
        next(it)
    y_ref = next(it)
    fin_ref = next(it) if has_final else None
    sloc_scr, sin_scr = next(it), next(it)

    nc = seq // CHUNK
    g0 = pl.program_id(1) * gs
    rows = SSD_R * SSD_P

    src = lax.broadcasted_iota(jnp.int32, (128, 128), 0)
    dst = lax.broadcasted_iota(jnp.int32, (128, 128), 1)
    want = ((dst >> 2) & 1) * SSD_HEADS + (g0 + (dst >> 3)) * SSD_R + (dst & 3)
    sel = jnp.where((dst < 8 * gs) & (src == want), 1.0, 0.0).astype(BF16)
    dt_raw = sum(_dot(p, sel) for p in _split3(dt_ref[...]))
    u = dt_raw + bias_ref[...]
    dt = jnp.maximum(u, 0.0) + jnp.log1p(jnp.exp(-jnp.abs(u)))
    a = dt * nega_ref[...]

    li = lax.broadcasted_iota(jnp.int32, (CHUNK, CHUNK), 0)
    si = lax.broadcasted_iota(jnp.int32, (CHUNK, CHUNK), 1)
    tril = jnp.where(si <= li, 1.0, 0.0).astype(BF16)
    triu = jnp.where(si >= li, 1.0, 0.0).astype(BF16)
    pre, suf = [], []
    for c in range(nc):
        parts = _split3(a[c * CHUNK:(c + 1) * CHUNK])
        pre.append(sum(_dot(tril, p) for p in parts))
        suf.append(sum(_dot(triu, p) for p in parts))
    pre = jnp.concatenate(pre, axis=0)
    suf = jnp.concatenate(suf, axis=0)
    fwd_lane = (lax.broadcasted_iota(jnp.int32, (1, 128), 1) & SSD_R) == 0
    e1 = jnp.where(fwd_lane, pre, suf)
    e2 = jnp.where(fwd_lane, suf, pre) - a
    e1s = e1 * LOG2E
    e1s_t = e1s.T
    dt_t = dt.T
    ysc_t = jnp.exp(e1).T
    w2_t = jnp.exp(e2).T

    def expand(t, off, cols):
        return jnp.concatenate(
            [jnp.broadcast_to(t[off + h:off + h + 1, cols], (SSD_P, cols.stop - cols.start))
             for h in range(SSD_R)], axis=0)

    def chunk_decay(off, col):
        return expand(ysc_t, off, slice(col, col + 1))

    x_t = x_ref[...].T

    for gi in range(gs):
        fwd, bwd = 8 * gi, 8 * gi + SSD_R
        xg_t = x_t[gi * rows:(gi + 1) * rows]
        b_cols = slice(gi * SSD_N, (gi + 1) * SSD_N)
        y_cols = slice(gi * rows, (gi + 1) * rows)

        for c in range(nc):
            cols = slice(c * CHUNK, (c + 1) * CHUNK)
            xc = xg_t[:, cols]
            xw = jnp.concatenate([xc * (expand(dt_t, fwd, cols) * expand(w2_t, fwd, cols)),
                                  xc * (expand(dt_t, bwd, cols) * expand(w2_t, bwd, cols))], axis=0)
            sloc_scr[gi, c] = _dot(xw.astype(BF16), b_ref[cols, b_cols].astype(BF16))

        heads = slice(gi * SSD_R, (gi + 1) * SSD_R)
        s_f = init_ref[0, heads].reshape(rows, SSD_N) if has_init else None
        for c in range(nc):
            if s_f is not None:
                sin_scr[gi, c, 0:rows] = s_f
                s_f = s_f * chunk_decay(fwd, c * CHUNK + CHUNK - 1) + sloc_scr[gi, c, 0:rows]
            else:
                s_f = sloc_scr[gi, c, 0:rows]
        s_b = init_ref[1, heads].reshape(rows, SSD_N) if has_init else None
        for c in reversed(range(nc)):
            if s_b is not None:
                sin_scr[gi, c, rows:2 * rows] = s_b
                s_b = s_b * chunk_decay(bwd, c * CHUNK) + sloc_scr[gi, c, rows:2 * rows]
            else:
                s_b = sloc_scr[gi, c, rows:2 * rows]
        if has_final:
            fin_ref[0, heads] = s_f.reshape(SSD_R, SSD_P, SSD_N)
            fin_ref[1, heads] = s_b.reshape(SSD_R, SSD_P, SSD_N)

        for c in range(nc):
            cols = slice(c * CHUNK, (c + 1) * CHUNK)
            bc = b_ref[cols, b_cols].astype(BF16)
            cc = c_ref[cols, b_cols].astype(BF16)
            cb_t = _dot_nt(bc, cc)
            xc = xg_t[:, cols]
            xdf = xc * expand(dt_t, fwd, cols)
            xdb = xc * expand(dt_t, bwd, cols)
            entering = []
            if has_init or c > 0:
                entering.append((slice(0, rows), fwd))
            if has_init or c < nc - 1:
                entering.append((slice(rows, 2 * rows), bwd))
            off = None
            if entering:
                s_in = jnp.concatenate([sin_scr[gi, c, r] for r, _ in entering], axis=0)
                prod = _dot_nt(s_in.astype(BF16), cc)
                for k, (_, base) in enumerate(entering):
                    term = prod[k * rows:(k + 1) * rows] * expand(ysc_t, base, cols)
                    off = term if off is None else off + term
            y_heads = []
            for h in range(SSD_R):
                pf_col = e1s[cols, fwd + h:fwd + h + 1]
                rb_col = e1s[cols, bwd + h:bwd + h + 1]
                pf_row = e1s_t[fwd + h:fwd + h + 1, cols]
                rb_row = e1s_t[bwd + h:bwd + h + 1, cols]
                m_f = jnp.exp2(jnp.where(li <= si, pf_row - pf_col, NEG_INF)) * cb_t
                m_b = jnp.exp2(jnp.where(li >= si, rb_row - rb_col, NEG_INF)) * cb_t
                hs = slice(h * SSD_P, (h + 1) * SSD_P)
                lhs = jnp.concatenate([xdf[hs], xdb[hs]], axis=1).astype(BF16)
                rhs = jnp.concatenate([m_f, m_b], axis=0).astype(BF16)
                y_h = _dot(lhs, rhs)
                y_heads.append(y_h if off is None else y_h + off[hs])
            y_t = jnp.concatenate(y_heads, axis=0)
            y_ref[cols, y_cols] = y_t.T + x_ref[cols, y_cols] * d_ref[:, y_cols]


def _ssd_call(xbc, dt, dt_bias, a_log, d, state, fin_prev, layer, *, latent):
    seq = L_LAT if latent else L_CTX
    n_seq = N_LAT if latent else N_CTX
    gs = GS_LAT if latent else GS_CTX
    row0 = T_CTX // seq if latent else 0
    nc = seq // CHUNK
    xw, bw = gs * SSD_R * SSD_P, gs * SSD_N
    xcol0 = 0
    bcol0 = D_INNER // bw
    ccol0 = bcol0 + SSD_GROUPS // gs
    step_spec = lambda w: pl.BlockSpec((None, 1, w), lambda n, k: (k, 0, 0))
    in_specs = [
        pl.BlockSpec((seq, xw), lambda n, k: (row0 + n, xcol0 + k)),
        pl.BlockSpec((seq, bw), lambda n, k: (row0 + n, bcol0 + k)),
        pl.BlockSpec((seq, bw), lambda n, k: (row0 + n, ccol0 + k)),
        pl.BlockSpec((seq, 128), lambda n, k: (row0 + n, 0)),
        step_spec(128), step_spec(128), step_spec(xw),
    ]
    args = [xbc, xbc, xbc, dt, _step_lanes(dt_bias, gs), _step_lanes(-jnp.exp(a_log), gs),
            jnp.repeat(d, SSD_P).reshape(SSD_GROUPS // gs, 1, xw)]
    state_block = (None, None, 2, gs * SSD_R, SSD_P, SSD_N)
    aliases = {}
    if latent:
        in_specs.append(pl.BlockSpec(state_block, lambda n, k: (n, layer, 0, k, 0, 0)))
        args.append(state)
    elif fin_prev is not None:
        in_specs.append(pl.BlockSpec(memory_space=pl.ANY))
        args.append(fin_prev)
        aliases = {len(args) - 1: 1}
    y_shape = jax.ShapeDtypeStruct((n_seq * seq, D_INNER), F32)
    y_spec = pl.BlockSpec((seq, xw), lambda n, k: (n, k))
    if latent:
        out_shape, out_specs = y_shape, y_spec
    else:
        out_shape = (y_shape, jax.ShapeDtypeStruct((N_CTX, DEPTH // 2, 2, SSD_HEADS, SSD_P, SSD_N), F32))
        out_specs = (y_spec, pl.BlockSpec(state_block, lambda n, k: (n, layer, 0, k, 0, 0)))
    scratch = pltpu.VMEM((gs, nc, 2 * SSD_R * SSD_P, SSD_N), F32)
    return pl.pallas_call(
        functools.partial(_ssd_kernel, seq=seq, gs=gs, has_init=latent, has_final=not latent,
                          has_prev=bool(aliases)),
        out_shape=out_shape,
        grid=(n_seq, SSD_GROUPS // gs),
        in_specs=in_specs,
        out_specs=out_specs,
        input_output_aliases=aliases,
        scratch_shapes=[scratch, scratch],
        compiler_params=_cparams("arbitrary", "arbitrary"),
        name="ssd_scan_lat" if latent else "ssd_scan_ctx",
    )(*args)


def _step_lanes(p, gs):
    q = p.reshape(2, SSD_GROUPS, SSD_R).transpose(1, 0, 2).reshape(SSD_GROUPS // gs, 8 * gs)
    return jnp.pad(q, ((0, 0), (0, 128 - 8 * gs))).reshape(SSD_GROUPS // gs, 1, 128)


def kernel(x_prompt, x_sample, cache_k, cache_v, state_ssm, c, c_ctx, ada_w, ada_b, norm_mix_g, norm_ffn_g,
           ffn_w_gate_up, ffn_w_down, na_w_qkv, na_w_o, na_rpb, ssd_w_in, ssd_conv_w, ssd_conv_b,
           ssd_dt_bias, ssd_a_log, ssd_d, ssd_norm_g, ssd_w_out, final_norm_g):
    cond = jnp.zeros((N_COND, D), F32).at[0].set(c_ctx).at[1:1 + N_LAT].set(c)
    mods = _ada_call(cond, ada_w, ada_b).reshape(DEPTH, N_COND, 6, 1, D)
    x = (x_prompt.reshape(T_CTX, D), x_sample.reshape(T_LAT, D))
    final_g = final_norm_g.reshape(1, D)
    w_qkv, w_o = na_w_qkv.astype(BF16), na_w_o.astype(BF16)
    w_in, w_out = ssd_w_in.astype(BF16), ssd_w_out.astype(BF16)
    w_gu, w_d = ffn_w_gate_up.astype(BF16), ffn_w_down.astype(BF16)
    conv_b = ssd_conv_b.reshape(DEPTH // 2, 1, CONV_DIM)
    cache_k_t, cache_v_t = cache_k.swapaxes(-1, -2), cache_v.swapaxes(-1, -2)

    new_kv, new_s = None, None
    for i in range(DEPTH):
        j = i // 2
        m = [mods[i, :, t] for t in range(6)]
        g_mix = norm_mix_g[i].reshape(1, D)
        if i % 2 == 0:
            qkv = _qkv_call(x, g_mix, m[0], m[1], w_qkv, j)
            o_ctx, *new_kv = _na_ctx_call(qkv, new_kv, j)
            t_even, t_odd = _bias_call(na_rpb[j])
            o_lat = _na_lat_call(qkv, cache_k_t, cache_v_t, j, t_even, t_odd)
            x = _oproj_call(o_ctx, o_lat, x, m[2], w_o, j)
        else:
            w_dt = jnp.pad(w_in[j, :, ZX_DIM:], ((0, 0), (0, 128 - 2 * SSD_HEADS)))
            z, xbc, dt = _inproj_call(x, g_mix, m[0], m[1], w_in, w_dt, ssd_conv_w, conv_b, j)
            scan_args = (xbc, dt, ssd_dt_bias[j], ssd_a_log[j], ssd_d[j])
            y_ctx, new_s = _ssd_call(*scan_args, None, new_s, j, latent=False)
            y_lat = _ssd_call(*scan_args, state_ssm, None, j, latent=True)
            x = _ssd_out_call(y_ctx, y_lat, z, x, m[2], ssd_norm_g[j].reshape(1, D_INNER), w_out, j)
        ffn_args = (x, norm_ffn_g[i].reshape(1, D), m[3], m[4], m[5], w_gu, w_d, final_g, i)
        if i < DEPTH - 1:
            x = _ffn_call(*ffn_args, final_norm=False)

    ctx_tiles = T_CTX // TM_FFN
    y_prompt = _ffn_call(*ffn_args, final_norm=True, tile0=0, n_tiles=ctx_tiles)
    y_sample = _ffn_call(*ffn_args, final_norm=True, tile0=ctx_tiles, n_tiles=T_LAT // TM_FFN)
    return (y_prompt.reshape(N_CTX, L_CTX, D), y_sample.reshape(N_LAT, L_LAT, D), new_kv[0], new_kv[1], new_s)
```

```python
import functools

import jax
import jax.numpy as jnp
from jax import lax
from jax.experimental import pallas as pl
from jax.experimental.pallas import tpu as pltpu

F32 = jnp.float32
BF16 = jnp.bfloat16
ACT_DTYPE = BF16

D = 1024
N_CTX, L_CTX = 32, 256
N_LAT, L_LAT = 4, 1024
T_CTX = N_CTX * L_CTX
T_LAT = N_LAT * L_LAT
T_ALL = T_CTX + T_LAT
DEPTH = 4
N_COND = 8
PAST = 512
GRID_W = 64
GRID_H = L_LAT // GRID_W
HEADS = 16
HEAD_DIM = 64
HEAD_PAIRS = HEADS // 2
WIN_H, WIN_W = 8, 16
RPB_H, RPB_W = 2 * WIN_H - 1, 2 * WIN_W - 1
D_INNER = 2048
SSD_HEADS = 32
SSD_P = 64
SSD_GROUPS = 8
SSD_R = SSD_HEADS // SSD_GROUPS
SSD_N = 128
CHUNK = 128
CONV_DIM = D_INNER + 2 * SSD_GROUPS * SSD_N
ZX_DIM = D_INNER + CONV_DIM
D_FF = 2816
EPS = 1e-6
NEG_INF = float("-inf")
LOG2E = 1.4426950408889634

TM = 1024
TM_FFN = 1024
MXU_DIM = 256
FF_CUTS = (0, (D_FF // MXU_DIM + 1) // 2 * MXU_DIM, D_FF)
TN_QKV = 1024
TN_IN = 1024
MM_COLS = 512
TM_OUT = 512
GS_CTX = 8
GS_LAT = 4
BIAS_BLOCKS = 18
VMEM_LIMIT = 56 * 1024 * 1024


def _cparams(*sem):
    return pltpu.CompilerParams(dimension_semantics=sem, vmem_limit_bytes=VMEM_LIMIT)


def _cond_of_tile(i, tm):
    return jnp.maximum((i * tm) // L_LAT - (T_CTX // L_LAT - 1), 0)


def _silu(x):
    return x * (0.5 * jnp.tanh(0.5 * x) + 0.5)


def _rms(x, g):
    return x * lax.rsqrt(jnp.mean(x * x, axis=-1, keepdims=True) + EPS) * g


def _norm_mod(x, g, shift, scale):
    return _rms(x, g) * (1.0 + scale) + shift


def _dot(a, b):
    return jnp.dot(a, b, preferred_element_type=F32)


def _dot_nt(a, b):
    return lax.dot_general(a, b, (((1,), (1,)), ((), ())), preferred_element_type=F32)


def _split3(x):
    hi = x.astype(BF16)
    r = x - hi.astype(F32)
    mid = r.astype(BF16)
    lo = (r - mid.astype(F32)).astype(BF16)
    return hi, mid, lo


def _ada_kernel(c_ref, w_ref, b_ref, o_ref):
    s = _silu(c_ref[...]).astype(BF16)
    o_ref[...] = _dot(s, w_ref[...].astype(BF16)) + b_ref[...]


def _ada_call(cond, ada_w, ada_b):
    tn = 1536
    return pl.pallas_call(
        _ada_kernel,
        out_shape=jax.ShapeDtypeStruct((DEPTH, N_COND, 6 * D), F32),
        grid=(DEPTH, 6 * D // tn),
        in_specs=[
            pl.BlockSpec((N_COND, D), lambda l, j: (0, 0)),
            pl.BlockSpec((None, D, tn), lambda l, j: (l, 0, j)),
            pl.BlockSpec((None, 1, tn), lambda l, j: (l, 0, j)),
        ],
        out_specs=pl.BlockSpec((None, N_COND, tn), lambda l, j: (l, 0, j)),
        compiler_params=_cparams("arbitrary", "arbitrary"),
        name="ada",
    )(cond, ada_w, ada_b.reshape(DEPTH, 1, 6 * D))


def _mod_spec(tm, tile0=0):
    return pl.BlockSpec((None, 1, D), lambda i, *_: (_cond_of_tile(i + tile0, tm), 0, 0))


def _stream_specs(streams, tm, width):
    n_ctx = T_CTX // tm
    lat0 = 0 if isinstance(streams, tuple) else n_ctx
    arrays = streams if isinstance(streams, tuple) else (streams, streams)
    specs = [pl.BlockSpec((tm, width), lambda i, *_: (jnp.minimum(i, n_ctx - 1), 0)),
             pl.BlockSpec((tm, width), lambda i, *_: (lat0 + jnp.maximum(i - n_ctx, 0), 0))]
    return list(arrays), specs


def _on_stream(tm, fn, ctx_ref, lat_ref, extra=True):
    is_ctx = pl.program_id(0) < T_CTX // tm
    pl.when(is_ctx & extra)(lambda: fn(ctx_ref))
    pl.when(jnp.logical_not(is_ctx) & extra)(lambda: fn(lat_ref))


def _qkv_kernel(xc_ref, xl_ref, g_ref, sh_ref, sc_ref, w_ref, q_ref, kv_ref, h_scr):
    j = pl.program_id(1)

    def prologue(x_ref):
        h_scr[...] = _norm_mod(x_ref[...], g_ref[...], sh_ref[...], sc_ref[...]).astype(BF16)

    _on_stream(TM, prologue, xc_ref, xl_ref, j == 0)

    @pl.when(j == 0)
    def _():
        q_ref[...] = _dot(h_scr[...], w_ref[0]).astype(BF16)

    @pl.when(j > 0)
    def _():
        kv_ref[...] = _dot(h_scr[...], w_ref[j])


def _column_tiles(w_all, n_cols, tn):
    layers, k, _ = w_all.shape
    return w_all[:, :, :n_cols].reshape(layers, k, n_cols // tn, tn).transpose(0, 2, 1, 3)


def _qkv_call(x, g, shift, scale, w_tiles, layer):
    assert TN_QKV == D
    n_tiles = w_tiles.shape[1]
    x_args, x_specs = _stream_specs(x, TM, D)
    return pl.pallas_call(
        _qkv_kernel,
        out_shape=(jax.ShapeDtypeStruct((T_ALL, D), BF16), jax.ShapeDtypeStruct((T_ALL, 2 * D), F32)),
        grid=(T_ALL // TM, n_tiles),
        in_specs=x_specs + [
            pl.BlockSpec((1, D), lambda i, j: (0, 0)),
            _mod_spec(TM),
            _mod_spec(TM),
            pl.BlockSpec((None, n_tiles, D, TN_QKV), lambda i, j: (layer, 0, 0, 0),
                         pipeline_mode=pl.Buffered(1)),
        ],
        out_specs=(pl.BlockSpec((TM, D), lambda i, j: (i, 0)),
                   pl.BlockSpec((TM, TN_QKV), lambda i, j: (i, jnp.maximum(j - 1, 0)))),
        scratch_shapes=[pltpu.VMEM((TM, D), BF16)],
        compiler_params=_cparams("arbitrary", "arbitrary"),
        name="na_qkv",
    )(*x_args, g, shift, scale, w_tiles)


def _inproj_kernel(x_ref, g_ref, sh_ref, sc_ref, w_ref, wdt_ref, cw_ref, cb_ref, z_ref, xbc_ref, dt_ref,
                   h_scr, c_scr):
    i = pl.program_id(0)
    s = pl.program_id(1)
    n_z = D_INNER // TN_IN

    @pl.when((i == 0) & (s == 0))
    def _():
        c_scr[0:8, :] = jnp.zeros((8, TN_IN), F32)
        c_scr[TM + 8:TM + 16, :] = jnp.zeros((8, TN_IN), F32)

    @pl.when(s == 0)
    def _():
        h = _norm_mod(x_ref[...], g_ref[...], sh_ref[...], sc_ref[...]).astype(BF16)
        h_scr[...] = h
        dt_ref[...] = _dot(h, wdt_ref[...])

    @pl.when(s < n_z)
    def _():
        z_ref[...] = _dot(h_scr[...], w_ref[s]).astype(z_ref.dtype)

    @pl.when(s >= n_z)
    def _():
        seq = jnp.where(i < T_CTX // TM, L_CTX, L_LAT)
        pos = lax.broadcasted_iota(jnp.int32, (TM, 1), 0) & (seq - 1)
        for m0 in range(0, TN_IN, MM_COLS):
            cols = slice(m0, m0 + MM_COLS)
            acc = _dot(h_scr[...], w_ref[s, :, cols])
            c_scr[8:TM + 8, cols] = acc
            prev = jnp.where(pos == 0, 0.0, c_scr[7:TM + 7, cols])
            nxt = jnp.where(pos == seq - 1, 0.0, c_scr[9:TM + 9, cols])
            y = cw_ref[0:1, cols] * prev + cw_ref[1:2, cols] * acc + cw_ref[2:3, cols] * nxt + cb_ref[:, cols]
            xbc_ref[:, cols] = _silu(y)


def _inproj_call(x, g, shift, scale, w_all, w_dt, conv_w_all, conv_b_all, layer):
    n_z = D_INNER // TN_IN
    n_steps = ZX_DIM // TN_IN
    conv_tile = lambda s: jnp.maximum(s - n_z, 0)
    return pl.pallas_call(
        _inproj_kernel,
        out_shape=(jax.ShapeDtypeStruct((T_ALL, D_INNER), ACT_DTYPE),
                   jax.ShapeDtypeStruct((T_ALL, CONV_DIM), F32),
                   jax.ShapeDtypeStruct((T_ALL, 128), F32)),
        grid=(T_ALL // TM, n_steps),
        in_specs=[
            pl.BlockSpec((TM, D), lambda i, s: (i, 0)),
            pl.BlockSpec((1, D), lambda i, s: (0, 0)),
            _mod_spec(TM),
            _mod_spec(TM),
            pl.BlockSpec((None, n_steps, D, TN_IN), lambda i, s: (layer, 0, 0, 0), pipeline_mode=pl.Buffered(1)),
            pl.BlockSpec((D, 128), lambda i, s: (0, 0)),
            pl.BlockSpec((None, 3, TN_IN), lambda i, s: (layer, 0, conv_tile(s))),
            pl.BlockSpec((None, 1, TN_IN), lambda i, s: (layer, 0, conv_tile(s))),
        ],
        out_specs=(pl.BlockSpec((TM, TN_IN), lambda i, s: (i, jnp.minimum(s, n_z - 1))),
                   pl.BlockSpec((TM, TN_IN), lambda i, s: (i, conv_tile(s))),
                   pl.BlockSpec((TM, 128), lambda i, s: (i, 0))),
        scratch_shapes=[pltpu.VMEM((TM, D), BF16), pltpu.VMEM((TM + 16, TN_IN), F32)],
        compiler_params=_cparams("arbitrary", "arbitrary"),
        name="ssd_inproj",
    )(x, g, shift, scale, w_all, w_dt, conv_w_all, conv_b_all)


def _oproj_kernel(ac_ref, al_ref, xc_ref, xl_ref, gt_ref, w_ref, o_ref):
    def run(refs):
        a_ref, x_ref = refs
        o_ref[...] = x_ref[...] + gt_ref[...] * _dot(a_ref[...], w_ref[...])

    _on_stream(TM_OUT, run, (ac_ref, xc_ref), (al_ref, xl_ref))


def _oproj_call(a_ctx, a_lat, x, gate, w_all, layer):
    a_args, a_specs = _stream_specs((a_ctx, a_lat), TM_OUT, D)
    x_args, x_specs = _stream_specs(x, TM_OUT, D)
    return pl.pallas_call(
        _oproj_kernel,
        out_shape=jax.ShapeDtypeStruct((T_ALL, D), F32),
        grid=(T_ALL // TM_OUT,),
        in_specs=a_specs + x_specs + [
            _mod_spec(TM_OUT),
            pl.BlockSpec((None, D, D), lambda i: (layer, 0, 0)),
        ],
        out_specs=pl.BlockSpec((TM_OUT, D), lambda i: (i, 0)),
        compiler_params=_cparams("arbitrary"),
        name="na_oproj",
    )(*a_args, *x_args, gate, w_all)


def _ssd_out_kernel(yc_ref, yl_ref, z_ref, x_ref, gt_ref, ng_ref, w_ref, o_ref):
    def run(y_ref):
        v = y_ref[...].astype(F32) * _silu(z_ref[...].astype(F32))
        v = _rms(v, ng_ref[...]).astype(BF16)
        o_ref[...] = x_ref[...] + gt_ref[...] * _dot(v, w_ref[...])

    _on_stream(TM_OUT, run, yc_ref, yl_ref)


def _ssd_out_call(y_ctx, y_lat, z, x, gate, norm_g, w_all, layer):
    y_args, y_specs = _stream_specs((y_ctx, y_lat), TM_OUT, D_INNER)
    return pl.pallas_call(
        _ssd_out_kernel,
        out_shape=jax.ShapeDtypeStruct((T_ALL, D), F32),
        grid=(T_ALL // TM_OUT,),
        in_specs=y_specs + [
            pl.BlockSpec((TM_OUT, D_INNER), lambda i: (i, 0)),
            pl.BlockSpec((TM_OUT, D), lambda i: (i, 0)),
            _mod_spec(TM_OUT),
            pl.BlockSpec((1, D_INNER), lambda i: (0, 0)),
            pl.BlockSpec((None, D_INNER, D), lambda i: (layer, 0, 0)),
        ],
        out_specs=pl.BlockSpec((TM_OUT, D), lambda i: (i, 0)),
        compiler_params=_cparams("arbitrary"),
        name="ssd_out",
    )(*y_args, z, x, gate, norm_g, w_all)


def _ffn_kernel(x_ref, g_ref, sh_ref, sc_ref, gt_ref, wgu_ref, wd_ref, fg_ref, o_ref, *, final_norm):
    half = TM_FFN // 2
    for r in range(2):
        rows = slice(r * half, (r + 1) * half)
        h = _norm_mod(x_ref[rows, :], g_ref[...], sh_ref[...], sc_ref[...]).astype(BF16)
        acc = None
        for lo, hi in zip(FF_CUTS[:-1], FF_CUTS[1:]):
            gate_cols = slice(lo, hi)
            up_cols = slice(D_FF + lo, D_FF + hi)
            act = (_silu(_dot(h, wgu_ref[:, gate_cols])) * _dot(h, wgu_ref[:, up_cols])).astype(BF16)
            part = _dot(act, wd_ref[gate_cols, :])
            acc = part if acc is None else acc + part
        y = x_ref[rows, :] + gt_ref[...] * acc
        if final_norm:
            y = _rms(y, fg_ref[...])
        o_ref[rows, :] = y


def _ffn_call(x, g, shift, scale, gate, w_gu_all, w_d_all, final_g, layer, final_norm,
              tile0=0, n_tiles=T_ALL // TM_FFN):
    resident = dict(pipeline_mode=pl.Buffered(1))
    return pl.pallas_call(
        functools.partial(_ffn_kernel, final_norm=final_norm),
        out_shape=jax.ShapeDtypeStruct((n_tiles * TM_FFN, D), F32),
        grid=(n_tiles,),
        in_specs=[
            pl.BlockSpec((TM_FFN, D), lambda i: (i + tile0, 0)),
            pl.BlockSpec((1, D), lambda i: (0, 0)),
            _mod_spec(TM_FFN, tile0),
            _mod_spec(TM_FFN, tile0),
            _mod_spec(TM_FFN, tile0),
            pl.BlockSpec((None, D, 2 * D_FF), lambda i: (layer, 0, 0), **resident),
            pl.BlockSpec((None, D_FF, D), lambda i: (layer, 0, 0), **resident),
            pl.BlockSpec((1, D), lambda i: (0, 0)),
        ],
        out_specs=pl.BlockSpec((TM_FFN, D), lambda i: (i, 0)),
        compiler_params=_cparams("arbitrary"),
        name="ffn",
    )(x, g, shift, scale, gate, w_gu_all, w_d_all, final_g)


def _bias_kernel(rpb_ref, te_ref, to_ref):
    h = pl.program_id(0)
    qc = lax.broadcasted_iota(jnp.int32, (GRID_W, 128), 0)
    lane = lax.broadcasted_iota(jnp.int32, (GRID_W, 128), 1)
    kc = lane & (GRID_W - 1)
    upper = lane >= GRID_W
    idx = jnp.clip(kc - qc + (WIN_W - 1), 0, RPB_W - 1)
    start = jnp.clip(qc - WIN_W // 2, 0, GRID_W - WIN_W)
    in_win = (kc >= start) & (kc < start + WIN_W)

    def pair_tile(dr_lo, dr_hi):
        def ok(dr):
            return 0 <= dr < RPB_H

        acc = jnp.zeros((GRID_W, 128), F32)
        for d in range(RPB_W):
            lo = rpb_ref[h * (RPB_H * RPB_W) + dr_lo * RPB_W + d] if ok(dr_lo) else 0.0
            hi = rpb_ref[h * (RPB_H * RPB_W) + dr_hi * RPB_W + d] if ok(dr_hi) else 0.0
            acc = jnp.where(idx == d, jnp.where(upper, hi, lo), acc)
        if ok(dr_lo) and ok(dr_hi):
            valid = in_win
        elif ok(dr_lo):
            valid = in_win & jnp.logical_not(upper)
        elif ok(dr_hi):
            valid = in_win & upper
        else:
            return jnp.full((GRID_W, 128), NEG_INF, F32)
        return jnp.where(valid, acc, NEG_INF)

    for t in range(BIAS_BLOCKS // 2):
        te_ref[:, t * 128:(t + 1) * 128] = pair_tile(2 * t - 1, 2 * t)
        to_ref[:, t * 128:(t + 1) * 128] = pair_tile(2 * t - 2, 2 * t - 1)


def _bias_call(rpb):
    shape = jax.ShapeDtypeStruct((HEADS, GRID_W, BIAS_BLOCKS * GRID_W), F32)
    spec = pl.BlockSpec((None, GRID_W, BIAS_BLOCKS * GRID_W), lambda h: (h, 0, 0))
    return pl.pallas_call(
        _bias_kernel,
        out_shape=(shape, shape),
        grid=(HEADS,),
        in_specs=[pl.BlockSpec(memory_space=pltpu.SMEM)],
        out_specs=(spec, spec),
        compiler_params=_cparams("arbitrary"),
        name="na_bias",
    )(rpb.reshape(-1))


def _half_mask(e):
    lane = lax.broadcasted_iota(jnp.int32, (1, 128), 1)
    return (lane >= HEAD_DIM) if e else (lane < HEAD_DIM)


def _na_ctx_kernel(q_ref, kv_ref, *refs):
    o_ref, ko_ref, vo_ref = refs[-3:]
    scale = HEAD_DIM ** -0.5
    for hp in range(HEAD_PAIRS):
        q2 = q_ref[:, hp * 128:(hp + 1) * 128] * scale
        k2 = kv_ref[:, hp * 128:(hp + 1) * 128]
        v2 = kv_ref[:, D + hp * 128:D + (hp + 1) * 128]
        for e in range(2):
            ko_ref[2 * hp + e] = k2[:, e * HEAD_DIM:(e + 1) * HEAD_DIM]
            vo_ref[2 * hp + e] = v2[:, e * HEAD_DIM:(e + 1) * HEAD_DIM]
        kb = k2.astype(BF16)
        vb = v2.astype(BF16)
        outs = []
        for e in range(2):
            qm = jnp.where(_half_mask(e), q2, jnp.zeros_like(q2))
            s = _dot_nt(qm, kb)
            p = jnp.exp(s - jnp.max(s, axis=-1, keepdims=True))
            l = jnp.sum(p, axis=-1, keepdims=True)
            outs.append(_dot(p.astype(BF16), vb) / l)
        o_ref[:, hp * 128:(hp + 1) * 128] = jnp.where(_half_mask(0), outs[0], outs[1]).astype(BF16)


def _na_ctx_call(q, kv, kv_prev, layer):
    kv_shape = jax.ShapeDtypeStruct((N_CTX, (DEPTH + 1) // 2, HEADS, L_CTX, HEAD_DIM), F32)
    kv_spec = pl.BlockSpec((None, None, HEADS, L_CTX, HEAD_DIM), lambda n: (n, layer, 0, 0, 0))
    in_specs = [pl.BlockSpec((L_CTX, D), lambda n: (n, 0)), pl.BlockSpec((L_CTX, 2 * D), lambda n: (n, 0))]
    args = [q, kv]
    aliases = {}
    if kv_prev is not None:
        in_specs += [pl.BlockSpec(memory_space=pl.ANY)] * 2
        args += list(kv_prev)
        aliases = {2: 1, 3: 2}
    return pl.pallas_call(
        _na_ctx_kernel,
        out_shape=(jax.ShapeDtypeStruct((T_CTX, D), BF16), kv_shape, kv_shape),
        grid=(N_CTX,),
        in_specs=in_specs,
        out_specs=(pl.BlockSpec((L_CTX, D), lambda n: (n, 0)), kv_spec, kv_spec),
        input_output_aliases=aliases,
        compiler_params=_cparams("arbitrary"),
        name="na_ctx",
    )(*args)


def _window_of_row(r):
    r0 = min(max(r - WIN_H // 2, 0), GRID_H - WIN_H)
    r0a = r0 & ~1
    rows = WIN_H if r0a == r0 else WIN_H + 2
    return r0, r0a, rows


def _na_bands():
    bands = []
    for rs in (range(0, GRID_H // 2), range(GRID_H // 2, GRID_H)):
        lo = min(_window_of_row(r)[1] for r in rs)
        hi = max(_window_of_row(r)[1] + _window_of_row(r)[2] for r in rs)
        bands.append((slice(rs[0] * GRID_W, (rs[-1] + 1) * GRID_W), slice(lo * GRID_W, hi * GRID_W)))
    return tuple(bands)


_NA_BANDS = _na_bands()


def _na_lat_kernel(q_ref, k_ref, v_ref, kc_ref, vc_ref, te_ref, to_ref, o_ref, *scratch):
    @pl.when((pl.program_id(0) == 0) & (pl.program_id(1) == 0))
    def _():
        scratch[2][...] = jnp.zeros_like(scratch[2])

    q2 = q_ref[...] * (HEAD_DIM ** -0.5)
    kb = k_ref[...].astype(BF16)
    vb = v_ref[...].astype(BF16)
    kc_t = jnp.concatenate([kc_ref[0], kc_ref[1]], axis=0).astype(BF16)
    vc_t = jnp.concatenate([vc_ref[0], vc_ref[1]], axis=0).astype(BF16)
    outs = []
    for e in range(2):
        s_scr, sc_scr, p_scr, pc_scr, li_scr = (ref.at[e] for ref in scratch)
        qm = jnp.where(_half_mask(e), q2, jnp.zeros_like(q2))
        for q_rows, k_rows in _NA_BANDS:
            s_scr[q_rows, k_rows] = _dot_nt(qm[q_rows], kb[k_rows])
        sc_scr[...] = _dot(qm, kc_t)
        for r in range(GRID_H):
            r0, r0a, nrows = _window_of_row(r)
            width = nrows * GRID_W
            blk = r0a - r + WIN_H
            t_ref, t_off = (te_ref, blk * GRID_W) if r % 2 == 0 else (to_ref, (blk + 1) * GRID_W)
            rows = slice(r * GRID_W, (r + 1) * GRID_W)
            cols = slice(r0a * GRID_W, r0a * GRID_W + width)
            sl = s_scr[rows, cols] + t_ref[e, :, t_off:t_off + width]
            if nrows != WIN_H:
                lane = lax.broadcasted_iota(jnp.int32, (1, width), 1)
                sl = jnp.where((lane < GRID_W) | (lane >= width - GRID_W), NEG_INF, sl)
            sc = sc_scr[rows, :]
            m = jnp.maximum(jnp.max(sl, axis=-1, keepdims=True), jnp.max(sc, axis=-1, keepdims=True))
            p_loc = jnp.exp(sl - m)
            p_ctx = jnp.exp(sc - m)
            l = jnp.sum(p_loc, axis=-1, keepdims=True) + jnp.sum(p_ctx, axis=-1, keepdims=True)
            p_scr[rows, cols] = p_loc.astype(BF16)
            pc_scr[rows, :] = p_ctx.astype(BF16)
            li_scr[rows, :] = jnp.broadcast_to(1.0 / l, (GRID_W, 128))
        o_loc = jnp.concatenate([_dot(p_scr[q_rows, k_rows], vb[k_rows]) for q_rows, k_rows in _NA_BANDS], axis=0)
        outs.append((o_loc + _dot_nt(pc_scr[...], vc_t)) * li_scr[...])
    o_ref[...] = jnp.where(_half_mask(0), outs[0], outs[1]).astype(BF16)


def _na_lat_call(q, kv, cache_k, cache_v, layer, t_even, t_odd):
    row0 = T_CTX // L_LAT
    ctx_spec = pl.BlockSpec((None, None, 2, HEAD_DIM, PAST), lambda b, hp: (b, layer, hp, 0, 0))
    t_spec = pl.BlockSpec((2, GRID_W, BIAS_BLOCKS * GRID_W), lambda b, hp: (hp, 0, 0))
    return pl.pallas_call(
        _na_lat_kernel,
        out_shape=jax.ShapeDtypeStruct((T_LAT, D), BF16),
        grid=(N_LAT, HEAD_PAIRS),
        in_specs=[
            pl.BlockSpec((L_LAT, 128), lambda b, hp: (row0 + b, hp)),
            pl.BlockSpec((L_LAT, 128), lambda b, hp: (row0 + b, hp)),
            pl.BlockSpec((L_LAT, 128), lambda b, hp: (row0 + b, HEAD_PAIRS + hp)),
            ctx_spec, ctx_spec, t_spec, t_spec,
        ],
        out_specs=pl.BlockSpec((L_LAT, 128), lambda b, hp: (b, hp)),
        scratch_shapes=[
            pltpu.VMEM((2, L_LAT, L_LAT), F32),
            pltpu.VMEM((2, L_LAT, PAST), F32),
            pltpu.VMEM((2, L_LAT, L_LAT), BF16),
            pltpu.VMEM((2, L_LAT, PAST), BF16),
            pltpu.VMEM((2, L_LAT, 128), F32),
        ],
        compiler_params=_cparams("arbitrary", "arbitrary"),
        name="na_lat",
    )(q, kv, kv, cache_k, cache_v, t_even, t_odd)


def _ssd_kernel(*refs, seq, gs, has_init, has_final, has_prev):
    it = iter(refs)
    x_ref, b_ref, c_ref, dt_ref, bias_ref, nega_ref, d_ref = (next(it) for _ in range(7))
    init_ref = next(it) if has_init else None
    if has_prev:
        next(it)
    y_ref = next(it)
    fin_ref = next(it) if has_final else None
    sloc_scr, sin_scr = next(it), next(it)

    nc = seq // CHUNK
    g0 = pl.program_id(1) * gs
    rows = SSD_R * SSD_P

    src = lax.broadcasted_iota(jnp.int32, (128, 128), 0)
    dst = lax.broadcasted_iota(jnp.int32, (128, 128), 1)
    want = ((dst >> 2) & 1) * SSD_HEADS + (g0 + (dst >> 3)) * SSD_R + (dst & 3)
    sel = jnp.where((dst < 8 * gs) & (src == want), 1.0, 0.0).astype(BF16)
    dt_raw = sum(_dot(p, sel) for p in _split3(dt_ref[...]))
    u = dt_raw + bias_ref[...]
    dt = jnp.maximum(u, 0.0) + jnp.log1p(jnp.exp(-jnp.abs(u)))
    a = dt * nega_ref[...]

    li = lax.broadcasted_iota(jnp.int32, (CHUNK, CHUNK), 0)
    si = lax.broadcasted_iota(jnp.int32, (CHUNK, CHUNK), 1)
    tril = jnp.where(si <= li, 1.0, 0.0).astype(BF16)
    triu = jnp.where(si >= li, 1.0, 0.0).astype(BF16)
    pre, suf = [], []
    for c in range(nc):
        parts = _split3(a[c * CHUNK:(c + 1) * CHUNK])
        pre.append(sum(_dot(tril, p) for p in parts))
        suf.append(sum(_dot(triu, p) for p in parts))
    pre = jnp.concatenate(pre, axis=0)
    suf = jnp.concatenate(suf, axis=0)
    fwd_lane = (lax.broadcasted_iota(jnp.int32, (1, 128), 1) & SSD_R) == 0
    e1 = jnp.where(fwd_lane, pre, suf)
    e2 = jnp.where(fwd_lane, suf, pre) - a
    e1s = e1 * LOG2E
    e1s_t = e1s.T
    dt_t = dt.T
    ysc_t = jnp.exp(e1).T
    w2_t = jnp.exp(e2).T

    def expand(t, off, cols):
        return jnp.concatenate(
            [jnp.broadcast_to(t[off + h:off + h + 1, cols], (SSD_P, cols.stop - cols.start))
             for h in range(SSD_R)], axis=0)

    def chunk_decay(off, col):
        return expand(ysc_t, off, slice(col, col + 1))

    x_t = x_ref[...].T

    for gi in range(gs):
        fwd, bwd = 8 * gi, 8 * gi + SSD_R
        xg_t = x_t[gi * rows:(gi + 1) * rows]
        b_cols = slice(gi * SSD_N, (gi + 1) * SSD_N)
        y_cols = slice(gi * rows, (gi + 1) * rows)

        for c in range(nc):
            cols = slice(c * CHUNK, (c + 1) * CHUNK)
            xc = xg_t[:, cols]
            xw = jnp.concatenate([xc * (expand(dt_t, fwd, cols) * expand(w2_t, fwd, cols)),
                                  xc * (expand(dt_t, bwd, cols) * expand(w2_t, bwd, cols))], axis=0)
            sloc_scr[gi, c] = _dot(xw.astype(BF16), b_ref[cols, b_cols].astype(BF16))

        heads = slice(gi * SSD_R, (gi + 1) * SSD_R)
        s_f = init_ref[0, heads].reshape(rows, SSD_N) if has_init else None
        for c in range(nc):
            if s_f is not None:
                sin_scr[gi, c, 0:rows] = s_f
                s_f = s_f * chunk_decay(fwd, c * CHUNK + CHUNK - 1) + sloc_scr[gi, c, 0:rows]
            else:
                s_f = sloc_scr[gi, c, 0:rows]
        s_b = init_ref[1, heads].reshape(rows, SSD_N) if has_init else None
        for c in reversed(range(nc)):
            if s_b is not None:
                sin_scr[gi, c, rows:2 * rows] = s_b
                s_b = s_b * chunk_decay(bwd, c * CHUNK) + sloc_scr[gi, c, rows:2 * rows]
            else:
                s_b = sloc_scr[gi, c, rows:2 * rows]
        if has_final:
            fin_ref[0, heads] = s_f.reshape(SSD_R, SSD_P, SSD_N)
            fin_ref[1, heads] = s_b.reshape(SSD_R, SSD_P, SSD_N)

        for c in range(nc):
            cols = slice(c * CHUNK, (c + 1) * CHUNK)
            bc = b_ref[cols, b_cols].astype(BF16)
            cc = c_ref[cols, b_cols].astype(BF16)
            cb_t = _dot_nt(bc, cc)
            xc = xg_t[:, cols]
            xdf = xc * expand(dt_t, fwd, cols)
            xdb = xc * expand(dt_t, bwd, cols)
            entering = []
            if has_init or c > 0:
                entering.append((slice(0, rows), fwd))
            if has_init or c < nc - 1:
                entering.append((slice(rows, 2 * rows), bwd))
            off = None
            if entering:
                s_in = jnp.concatenate([sin_scr[gi, c, r] for r, _ in entering], axis=0)
                prod = _dot_nt(s_in.astype(BF16), cc)
                for k, (_, base) in enumerate(entering):
                    term = prod[k * rows:(k + 1) * rows] * expand(ysc_t, base, cols)
                    off = term if off is None else off + term
            y_heads = []
            for h in range(SSD_R):
                pf_col = e1s[cols, fwd + h:fwd + h + 1]
                rb_col = e1s[cols, bwd + h:bwd + h + 1]
                pf_row = e1s_t[fwd + h:fwd + h + 1, cols]
                rb_row = e1s_t[bwd + h:bwd + h + 1, cols]
                m_f = jnp.exp2(jnp.where(li <= si, pf_row - pf_col, NEG_INF)) * cb_t
                m_b = jnp.exp2(jnp.where(li >= si, rb_row - rb_col, NEG_INF)) * cb_t
                hs = slice(h * SSD_P, (h + 1) * SSD_P)
                lhs = jnp.concatenate([xdf[hs], xdb[hs]], axis=1).astype(BF16)
                rhs = jnp.concatenate([m_f, m_b], axis=0).astype(BF16)
                y_h = _dot(lhs, rhs)
                y_heads.append(y_h if off is None else y_h + off[hs])
            y_t = jnp.concatenate(y_heads, axis=0)
            y_ref[cols, y_cols] = (y_t.T + x_ref[cols, y_cols] * d_ref[:, y_cols]).astype(y_ref.dtype)


def _ssd_call(xbc, dt, dt_bias, a_log, d, state, fin_prev, layer, *, latent):
    seq = L_LAT if latent else L_CTX
    n_seq = N_LAT if latent else N_CTX
    gs = GS_LAT if latent else GS_CTX
    row0 = T_CTX // seq if latent else 0
    nc = seq // CHUNK
    xw, bw = gs * SSD_R * SSD_P, gs * SSD_N
    xcol0 = 0
    bcol0 = D_INNER // bw
    ccol0 = bcol0 + SSD_GROUPS // gs
    step_spec = lambda w: pl.BlockSpec((None, 1, w), lambda n, k: (k, 0, 0))
    in_specs = [
        pl.BlockSpec((seq, xw), lambda n, k: (row0 + n, xcol0 + k)),
        pl.BlockSpec((seq, bw), lambda n, k: (row0 + n, bcol0 + k)),
        pl.BlockSpec((seq, bw), lambda n, k: (row0 + n, ccol0 + k)),
        pl.BlockSpec((seq, 128), lambda n, k: (row0 + n, 0)),
        step_spec(128), step_spec(128), step_spec(xw),
    ]
    args = [xbc, xbc, xbc, dt, _step_lanes(dt_bias, gs), _step_lanes(-jnp.exp(a_log), gs),
            jnp.repeat(d, SSD_P).reshape(SSD_GROUPS // gs, 1, xw)]
    state_block = (None, None, 2, gs * SSD_R, SSD_P, SSD_N)
    aliases = {}
    if latent:
        in_specs.append(pl.BlockSpec(state_block, lambda n, k: (n, layer, 0, k, 0, 0)))
        args.append(state)
    elif fin_prev is not None:
        in_specs.append(pl.BlockSpec(memory_space=pl.ANY))
        args.append(fin_prev)
        aliases = {len(args) - 1: 1}
    y_shape = jax.ShapeDtypeStruct((n_seq * seq, D_INNER), ACT_DTYPE)
    y_spec = pl.BlockSpec((seq, xw), lambda n, k: (n, k))
    if latent:
        out_shape, out_specs = y_shape, y_spec
    else:
        out_shape = (y_shape, jax.ShapeDtypeStruct((N_CTX, DEPTH // 2, 2, SSD_HEADS, SSD_P, SSD_N), F32))
        out_specs = (y_spec, pl.BlockSpec(state_block, lambda n, k: (n, layer, 0, k, 0, 0)))
    scratch = pltpu.VMEM((gs, nc, 2 * SSD_R * SSD_P, SSD_N), F32)
    return pl.pallas_call(
        functools.partial(_ssd_kernel, seq=seq, gs=gs, has_init=latent, has_final=not latent,
                          has_prev=bool(aliases)),
        out_shape=out_shape,
        grid=(n_seq, SSD_GROUPS // gs),
        in_specs=in_specs,
        out_specs=out_specs,
        input_output_aliases=aliases,
        scratch_shapes=[scratch, scratch],
        compiler_params=_cparams("arbitrary", "arbitrary"),
        name="ssd_scan_lat" if latent else "ssd_scan_ctx",
    )(*args)


def _step_lanes(p, gs):
    q = p.reshape(2, SSD_GROUPS, SSD_R).transpose(1, 0, 2).reshape(SSD_GROUPS // gs, 8 * gs)
    return jnp.pad(q, ((0, 0), (0, 128 - 8 * gs))).reshape(SSD_GROUPS // gs, 1, 128)


def kernel(x_prompt, x_sample, cache_k, cache_v, state_ssm, c, c_ctx, ada_w, ada_b, norm_mix_g, norm_ffn_g,
           ffn_w_gate_up, ffn_w_down, na_w_qkv, na_w_o, na_rpb, ssd_w_in, ssd_conv_w, ssd_conv_b,
           ssd_dt_bias, ssd_a_log, ssd_d, ssd_norm_g, ssd_w_out, final_norm_g):
    cond = jnp.zeros((N_COND, D), F32).at[0].set(c_ctx).at[1:1 + N_LAT].set(c)
    mods = _ada_call(cond, ada_w, ada_b).reshape(DEPTH, N_COND, 6, 1, D)
    x = (x_prompt.reshape(T_CTX, D), x_sample.reshape(T_LAT, D))
    final_g = final_norm_g.reshape(1, D)
    w_qkv, w_o = _column_tiles(na_w_qkv.astype(BF16), 3 * D, TN_QKV), na_w_o.astype(BF16)
    w_in, w_out = ssd_w_in.astype(BF16), ssd_w_out.astype(BF16)
    w_zx = _column_tiles(w_in, ZX_DIM, TN_IN)
    w_gu, w_d = ffn_w_gate_up.astype(BF16), ffn_w_down.astype(BF16)
    conv_b = ssd_conv_b.reshape(DEPTH // 2, 1, CONV_DIM)
    cache_k_t, cache_v_t = cache_k.swapaxes(-1, -2), cache_v.swapaxes(-1, -2)

    new_kv, new_s = None, None
    for i in range(DEPTH):
        j = i // 2
        m = [mods[i, :, t] for t in range(6)]
        g_mix = norm_mix_g[i].reshape(1, D)
        if i % 2 == 0:
            q, kv = _qkv_call(x, g_mix, m[0], m[1], w_qkv, j)
            o_ctx, *new_kv = _na_ctx_call(q, kv, new_kv, j)
            t_even, t_odd = _bias_call(na_rpb[j])
            o_lat = _na_lat_call(q, kv, cache_k_t, cache_v_t, j, t_even, t_odd)
            x = _oproj_call(o_ctx, o_lat, x, m[2], w_o, j)
        else:
            w_dt = jnp.pad(w_in[j, :, ZX_DIM:], ((0, 0), (0, 128 - 2 * SSD_HEADS)))
            z, xbc, dt = _inproj_call(x, g_mix, m[0], m[1], w_zx, w_dt, ssd_conv_w, conv_b, j)
            scan_args = (xbc, dt, ssd_dt_bias[j], ssd_a_log[j], ssd_d[j])
            y_ctx, new_s = _ssd_call(*scan_args, None, new_s, j, latent=False)
            y_lat = _ssd_call(*scan_args, state_ssm, None, j, latent=True)
            x = _ssd_out_call(y_ctx, y_lat, z, x, m[2], ssd_norm_g[j].reshape(1, D_INNER), w_out, j)
        ffn_args = (x, norm_ffn_g[i].reshape(1, D), m[3], m[4], m[5], w_gu, w_d, final_g, i)
        if i < DEPTH - 1:
            x = _ffn_call(*ffn_args, final_norm=False)

    ctx_tiles = T_CTX // TM_FFN
    y_prompt = _ffn_call(*ffn_args, final_norm=True, tile0=0, n_tiles=ctx_tiles)
    y_sample = _ffn_call(*ffn_args, final_norm=True, tile0=ctx_tiles, n_tiles=T_LAT // TM_FFN)
    return (y_prompt.reshape(N_CTX, L_CTX, D), y_sample.reshape(N_LAT, L_LAT, D), new_kv[0], new_kv[1], new_s)
```

```python
import functools

import jax
import jax.numpy as jnp
from jax import lax
from jax.experimental import pallas as pl
from jax.experimental.pallas import tpu as pltpu

F32 = jnp.float32
BF16 = jnp.bfloat16

D = 1024
N_CTX, L_CTX = 32, 256
N_LAT, L_LAT = 4, 1024
T_CTX = N_CTX * L_CTX
T_LAT = N_LAT * L_LAT
T_ALL = T_CTX + T_LAT
DEPTH = 4
N_COND = 8
PAST = 512
GRID_W = 64
GRID_H = L_LAT // GRID_W
HEADS = 16
HEAD_DIM = 64
HEAD_PAIRS = HEADS // 2
WIN_H, WIN_W = 8, 16
RPB_H, RPB_W = 2 * WIN_H - 1, 2 * WIN_W - 1
D_INNER = 2048
SSD_HEADS = 32
SSD_P = 64
SSD_GROUPS = 8
SSD_R = SSD_HEADS // SSD_GROUPS
SSD_N = 128
CHUNK = 128
CONV_DIM = D_INNER + 2 * SSD_GROUPS * SSD_N
ZX_DIM = D_INNER + CONV_DIM
D_FF = 2816
EPS = 1e-6
NEG_INF = float("-inf")
LOG2E = 1.4426950408889634

TM = 1024
TM_FFN = 1024
MXU_DIM = 256
FF_CUTS = (0, (D_FF // MXU_DIM + 1) // 2 * MXU_DIM, D_FF)
TN_QKV = 1024
TN_IN = 1024
MM_COLS = 512
TM_OUT = 512
GS_CTX = 8
GS_LAT = 4
BIAS_BLOCKS = 18
VMEM_LIMIT = 56 * 1024 * 1024


def _cparams(*sem):
    return pltpu.CompilerParams(dimension_semantics=sem, vmem_limit_bytes=VMEM_LIMIT)


def _cond_of_tile(i, tm):
    return jnp.maximum((i * tm) // L_LAT - (T_CTX // L_LAT - 1), 0)


def _silu(x):
    return x * (0.5 * jnp.tanh(0.5 * x) + 0.5)


def _rms(x, g):
    return x * lax.rsqrt(jnp.mean(x * x, axis=-1, keepdims=True) + EPS) * g


def _norm_mod(x, g, shift, scale):
    return _rms(x, g) * (1.0 + scale) + shift


def _dot(a, b):
    return jnp.dot(a, b, preferred_element_type=F32)


def _dot_nt(a, b):
    return lax.dot_general(a, b, (((1,), (1,)), ((), ())), preferred_element_type=F32)


def _split3(x):
    hi = x.astype(BF16)
    r = x - hi.astype(F32)
    mid = r.astype(BF16)
    lo = (r - mid.astype(F32)).astype(BF16)
    return hi, mid, lo


def _ada_kernel(c_ref, w_ref, b_ref, o_ref):
    s = _silu(c_ref[...]).astype(BF16)
    o_ref[...] = _dot(s, w_ref[...].astype(BF16)) + b_ref[...]


def _ada_call(cond, ada_w, ada_b):
    tn = 1536
    return pl.pallas_call(
        _ada_kernel,
        out_shape=jax.ShapeDtypeStruct((DEPTH, N_COND, 6 * D), F32),
        grid=(DEPTH, 6 * D // tn),
        in_specs=[
            pl.BlockSpec((N_COND, D), lambda l, j: (0, 0)),
            pl.BlockSpec((None, D, tn), lambda l, j: (l, 0, j)),
            pl.BlockSpec((None, 1, tn), lambda l, j: (l, 0, j)),
        ],
        out_specs=pl.BlockSpec((None, N_COND, tn), lambda l, j: (l, 0, j)),
        compiler_params=_cparams("arbitrary", "arbitrary"),
        name="ada",
    )(cond, ada_w, ada_b.reshape(DEPTH, 1, 6 * D))


def _mod_spec(tm):
    return pl.BlockSpec((None, 1, D), lambda i, *_: (_cond_of_tile(i, tm), 0, 0))


def _stream_specs(streams, tm, width):
    n_ctx = T_CTX // tm
    lat0 = 0 if isinstance(streams, tuple) else n_ctx
    arrays = streams if isinstance(streams, tuple) else (streams, streams)
    specs = [pl.BlockSpec((tm, width), lambda i, *_: (jnp.minimum(i, n_ctx - 1), 0)),
             pl.BlockSpec((tm, width), lambda i, *_: (lat0 + jnp.maximum(i - n_ctx, 0), 0))]
    return list(arrays), specs


def _on_stream(tm, fn, ctx_ref, lat_ref, extra=True):
    is_ctx = pl.program_id(0) < T_CTX // tm
    pl.when(is_ctx & extra)(lambda: fn(ctx_ref))
    pl.when(jnp.logical_not(is_ctx) & extra)(lambda: fn(lat_ref))


def _qkv_kernel(xc_ref, xl_ref, g_ref, sh_ref, sc_ref, w_ref, o_ref, h_scr):
    def prologue(x_ref):
        h_scr[...] = _norm_mod(x_ref[...], g_ref[...], sh_ref[...], sc_ref[...]).astype(BF16)

    _on_stream(TM, prologue, xc_ref, xl_ref, pl.program_id(1) == 0)
    o_ref[...] = _dot(h_scr[...], w_ref[...])


def _qkv_call(x, g, shift, scale, w_all, layer):
    n = w_all.shape[2]
    x_args, x_specs = _stream_specs(x, TM, D)
    return pl.pallas_call(
        _qkv_kernel,
        out_shape=jax.ShapeDtypeStruct((T_ALL, n), F32),
        grid=(T_ALL // TM, n // TN_QKV),
        in_specs=x_specs + [
            pl.BlockSpec((1, D), lambda i, j: (0, 0)),
            _mod_spec(TM),
            _mod_spec(TM),
            pl.BlockSpec((None, D, TN_QKV), lambda i, j: (layer, 0, j)),
        ],
        out_specs=pl.BlockSpec((TM, TN_QKV), lambda i, j: (i, j)),
        scratch_shapes=[pltpu.VMEM((TM, D), BF16)],
        compiler_params=_cparams("arbitrary", "arbitrary"),
        name="na_qkv",
    )(*x_args, g, shift, scale, w_all)


def _inproj_kernel(x_ref, g_ref, sh_ref, sc_ref, w_ref, wdt_ref, cw_ref, cb_ref, z_ref, xbc_ref, dt_ref,
                   h_scr, c_scr):
    i = pl.program_id(0)
    s = pl.program_id(1)
    n_z = D_INNER // TN_IN

    @pl.when((i == 0) & (s == 0))
    def _():
        c_scr[0:8, :] = jnp.zeros((8, TN_IN), F32)
        c_scr[TM + 8:TM + 16, :] = jnp.zeros((8, TN_IN), F32)

    @pl.when(s == 0)
    def _():
        h = _norm_mod(x_ref[...], g_ref[...], sh_ref[...], sc_ref[...]).astype(BF16)
        h_scr[...] = h
        dt_ref[...] = _dot(h, wdt_ref[...])

    @pl.when(s < n_z)
    def _():
        z_ref[...] = _dot(h_scr[...], w_ref[...])

    @pl.when(s >= n_z)
    def _():
        seq = jnp.where(i < T_CTX // TM, L_CTX, L_LAT)
        pos = lax.broadcasted_iota(jnp.int32, (TM, 1), 0) & (seq - 1)
        for m0 in range(0, TN_IN, MM_COLS):
            cols = slice(m0, m0 + MM_COLS)
            acc = _dot(h_scr[...], w_ref[:, cols])
            c_scr[8:TM + 8, cols] = acc
            prev = jnp.where(pos == 0, 0.0, c_scr[7:TM + 7, cols])
            nxt = jnp.where(pos == seq - 1, 0.0, c_scr[9:TM + 9, cols])
            y = cw_ref[0:1, cols] * prev + cw_ref[1:2, cols] * acc + cw_ref[2:3, cols] * nxt + cb_ref[:, cols]
            xbc_ref[:, cols] = _silu(y)


def _inproj_call(x, g, shift, scale, w_all, w_dt, conv_w_all, conv_b_all, layer):
    n_z = D_INNER // TN_IN
    n_steps = ZX_DIM // TN_IN
    conv_tile = lambda s: jnp.maximum(s - n_z, 0)
    return pl.pallas_call(
        _inproj_kernel,
        out_shape=(jax.ShapeDtypeStruct((T_ALL, D_INNER), F32),
                   jax.ShapeDtypeStruct((T_ALL, CONV_DIM), F32),
                   jax.ShapeDtypeStruct((T_ALL, 128), F32)),
        grid=(T_ALL // TM, n_steps),
        in_specs=[
            pl.BlockSpec((TM, D), lambda i, s: (i, 0)),
            pl.BlockSpec((1, D), lambda i, s: (0, 0)),
            _mod_spec(TM),
            _mod_spec(TM),
            pl.BlockSpec((None, D, TN_IN), lambda i, s: (layer, 0, s)),
            pl.BlockSpec((D, 128), lambda i, s: (0, 0)),
            pl.BlockSpec((None, 3, TN_IN), lambda i, s: (layer, 0, conv_tile(s))),
            pl.BlockSpec((None, 1, TN_IN), lambda i, s: (layer, 0, conv_tile(s))),
        ],
        out_specs=(pl.BlockSpec((TM, TN_IN), lambda i, s: (i, jnp.minimum(s, n_z - 1))),
                   pl.BlockSpec((TM, TN_IN), lambda i, s: (i, conv_tile(s))),
                   pl.BlockSpec((TM, 128), lambda i, s: (i, 0))),
        scratch_shapes=[pltpu.VMEM((TM, D), BF16), pltpu.VMEM((TM + 16, TN_IN), F32)],
        compiler_params=_cparams("arbitrary", "arbitrary"),
        name="ssd_inproj",
    )(x, g, shift, scale, w_all, w_dt, conv_w_all, conv_b_all)


def _oproj_kernel(ac_ref, al_ref, xc_ref, xl_ref, gt_ref, w_ref, o_ref):
    def run(refs):
        a_ref, x_ref = refs
        o_ref[...] = x_ref[...] + gt_ref[...] * _dot(a_ref[...], w_ref[...])

    _on_stream(TM_OUT, run, (ac_ref, xc_ref), (al_ref, xl_ref))


def _oproj_call(a_ctx, a_lat, x, gate, w_all, layer):
    a_args, a_specs = _stream_specs((a_ctx, a_lat), TM_OUT, D)
    x_args, x_specs = _stream_specs(x, TM_OUT, D)
    return pl.pallas_call(
        _oproj_kernel,
        out_shape=jax.ShapeDtypeStruct((T_ALL, D), F32),
        grid=(T_ALL // TM_OUT,),
        in_specs=a_specs + x_specs + [
            _mod_spec(TM_OUT),
            pl.BlockSpec((None, D, D), lambda i: (layer, 0, 0)),
        ],
        out_specs=pl.BlockSpec((TM_OUT, D), lambda i: (i, 0)),
        compiler_params=_cparams("arbitrary"),
        name="na_oproj",
    )(*a_args, *x_args, gate, w_all)


def _ssd_out_kernel(yc_ref, yl_ref, z_ref, x_ref, gt_ref, ng_ref, w_ref, o_ref):
    def run(y_ref):
        v = y_ref[...] * _silu(z_ref[...])
        v = _rms(v, ng_ref[...]).astype(BF16)
        o_ref[...] = x_ref[...] + gt_ref[...] * _dot(v, w_ref[...])

    _on_stream(TM_OUT, run, yc_ref, yl_ref)


def _ssd_out_call(y_ctx, y_lat, z, x, gate, norm_g, w_all, layer):
    y_args, y_specs = _stream_specs((y_ctx, y_lat), TM_OUT, D_INNER)
    return pl.pallas_call(
        _ssd_out_kernel,
        out_shape=jax.ShapeDtypeStruct((T_ALL, D), F32),
        grid=(T_ALL // TM_OUT,),
        in_specs=y_specs + [
            pl.BlockSpec((TM_OUT, D_INNER), lambda i: (i, 0)),
            pl.BlockSpec((TM_OUT, D), lambda i: (i, 0)),
            _mod_spec(TM_OUT),
            pl.BlockSpec((1, D_INNER), lambda i: (0, 0)),
            pl.BlockSpec((None, D_INNER, D), lambda i: (layer, 0, 0)),
        ],
        out_specs=pl.BlockSpec((TM_OUT, D), lambda i: (i, 0)),
        compiler_params=_cparams("arbitrary"),
        name="ssd_out",
    )(*y_args, z, x, gate, norm_g, w_all)


def _ffn_kernel(x_ref, g_ref, sh_ref, sc_ref, gt_ref, wgu_ref, wd_ref, fg_ref, *o_refs, final_norm):
    half = TM_FFN // 2
    for r in range(2):
        rows = slice(r * half, (r + 1) * half)
        h = _norm_mod(x_ref[rows, :], g_ref[...], sh_ref[...], sc_ref[...]).astype(BF16)
        acc = None
        for lo, hi in zip(FF_CUTS[:-1], FF_CUTS[1:]):
            gate_cols = slice(lo, hi)
            up_cols = slice(D_FF + lo, D_FF + hi)
            act = (_silu(_dot(h, wgu_ref[:, gate_cols])) * _dot(h, wgu_ref[:, up_cols])).astype(BF16)
            part = _dot(act, wd_ref[gate_cols, :])
            acc = part if acc is None else acc + part
        y = x_ref[rows, :] + gt_ref[...] * acc
        if final_norm:
            y = _rms(y, fg_ref[...])
        if len(o_refs) == 1:
            o_refs[0][rows, :] = y
        else:
            def store(o_ref, y=y, rows=rows):
                o_ref[rows, :] = y

            _on_stream(TM_FFN, store, *o_refs)


def _ffn_call(x, g, shift, scale, gate, w_gu_all, w_d_all, final_g, layer, final_norm, split_out=False):
    resident = dict(pipeline_mode=pl.Buffered(1))
    n_ctx = T_CTX // TM_FFN
    if split_out:
        out_shape = (jax.ShapeDtypeStruct((T_CTX, D), F32), jax.ShapeDtypeStruct((T_LAT, D), F32))
        out_specs = (pl.BlockSpec((TM_FFN, D), lambda i: (jnp.minimum(i, n_ctx - 1), 0)),
                     pl.BlockSpec((TM_FFN, D), lambda i: (jnp.maximum(i - n_ctx, 0), 0)))
    else:
        out_shape = jax.ShapeDtypeStruct((T_ALL, D), F32)
        out_specs = pl.BlockSpec((TM_FFN, D), lambda i: (i, 0))
    return pl.pallas_call(
        functools.partial(_ffn_kernel, final_norm=final_norm),
        out_shape=out_shape,
        grid=(T_ALL // TM_FFN,),
        in_specs=[
            pl.BlockSpec((TM_FFN, D), lambda i: (i, 0)),
            pl.BlockSpec((1, D), lambda i: (0, 0)),
            _mod_spec(TM_FFN),
            _mod_spec(TM_FFN),
            _mod_spec(TM_FFN),
            pl.BlockSpec((None, D, 2 * D_FF), lambda i: (layer, 0, 0), **resident),
            pl.BlockSpec((None, D_FF, D), lambda i: (layer, 0, 0), **resident),
            pl.BlockSpec((1, D), lambda i: (0, 0)),
        ],
        out_specs=out_specs,
        compiler_params=_cparams("arbitrary"),
        name="ffn",
    )(x, g, shift, scale, gate, w_gu_all, w_d_all, final_g)


def _bias_kernel(rpb_ref, te_ref, to_ref):
    qc = lax.broadcasted_iota(jnp.int32, (GRID_W, 128), 0)
    lane = lax.broadcasted_iota(jnp.int32, (GRID_W, 128), 1)
    kc = lane & (GRID_W - 1)
    upper = lane >= GRID_W
    start = jnp.clip(qc - WIN_W // 2, 0, GRID_W - WIN_W)
    in_win = (kc >= start) & (kc < start + WIN_W)

    def block(dr, lane0):
        base = jnp.broadcast_to(rpb_ref[dr:dr + 1, :], (GRID_W, 128))
        return pltpu.roll(base, (lane0 - (WIN_W - 1)) % 128, 1, stride=1, stride_axis=0)

    def pair_tile(dr_lo, dr_hi):
        def ok(dr):
            return 0 <= dr < RPB_H

        if ok(dr_lo) and ok(dr_hi):
            return jnp.where(in_win, jnp.where(upper, block(dr_hi, GRID_W), block(dr_lo, 0)), NEG_INF)
        if ok(dr_lo):
            return jnp.where(in_win & jnp.logical_not(upper), block(dr_lo, 0), NEG_INF)
        if ok(dr_hi):
            return jnp.where(in_win & upper, block(dr_hi, GRID_W), NEG_INF)
        return jnp.full((GRID_W, 128), NEG_INF, F32)

    for t in range(BIAS_BLOCKS // 2):
        te_ref[:, t * 128:(t + 1) * 128] = pair_tile(2 * t - 1, 2 * t)
        to_ref[:, t * 128:(t + 1) * 128] = pair_tile(2 * t - 2, 2 * t - 1)


def _bias_call(rpb):
    shape = jax.ShapeDtypeStruct((HEADS, GRID_W, BIAS_BLOCKS * GRID_W), F32)
    spec = pl.BlockSpec((None, GRID_W, BIAS_BLOCKS * GRID_W), lambda h: (h, 0, 0))
    rows = jnp.pad(rpb, ((0, 0), (0, 16 - RPB_H), (0, 128 - RPB_W)))
    return pl.pallas_call(
        _bias_kernel,
        out_shape=(shape, shape),
        grid=(HEADS,),
        in_specs=[pl.BlockSpec((None, 16, 128), lambda h: (h, 0, 0))],
        out_specs=(spec, spec),
        compiler_params=_cparams("arbitrary"),
        name="na_bias",
    )(rows)


def _half_mask(e):
    lane = lax.broadcasted_iota(jnp.int32, (1, 128), 1)
    return (lane >= HEAD_DIM) if e else (lane < HEAD_DIM)


def _na_ctx_kernel(qkv_ref, *refs):
    o_ref, ko_ref, vo_ref = refs[-3:]
    scale = HEAD_DIM ** -0.5
    for hp in range(HEAD_PAIRS):
        q2 = qkv_ref[:, hp * 128:(hp + 1) * 128] * scale
        k2 = qkv_ref[:, D + hp * 128:D + (hp + 1) * 128]
        v2 = qkv_ref[:, 2 * D + hp * 128:2 * D + (hp + 1) * 128]
        for e in range(2):
            ko_ref[2 * hp + e] = k2[:, e * HEAD_DIM:(e + 1) * HEAD_DIM]
            vo_ref[2 * hp + e] = v2[:, e * HEAD_DIM:(e + 1) * HEAD_DIM]
        kb = k2.astype(BF16)
        vb = v2.astype(BF16)
        outs = []
        for e in range(2):
            qm = jnp.where(_half_mask(e), q2, 0.0).astype(BF16)
            s = _dot_nt(qm, kb)
            p = jnp.exp(s - jnp.max(s, axis=-1, keepdims=True))
            l = jnp.sum(p, axis=-1, keepdims=True)
            outs.append(_dot(p.astype(BF16), vb) / l)
        o_ref[:, hp * 128:(hp + 1) * 128] = jnp.where(_half_mask(0), outs[0], outs[1]).astype(BF16)


def _na_ctx_call(qkv, kv_prev, layer):
    kv_shape = jax.ShapeDtypeStruct((N_CTX, (DEPTH + 1) // 2, HEADS, L_CTX, HEAD_DIM), F32)
    kv_spec = pl.BlockSpec((None, None, HEADS, L_CTX, HEAD_DIM), lambda n: (n, layer, 0, 0, 0))
    in_specs = [pl.BlockSpec((L_CTX, 3 * D), lambda n: (n, 0))]
    args = [qkv]
    aliases = {}
    if kv_prev is not None:
        in_specs += [pl.BlockSpec(memory_space=pl.ANY)] * 2
        args += list(kv_prev)
        aliases = {1: 1, 2: 2}
    return pl.pallas_call(
        _na_ctx_kernel,
        out_shape=(jax.ShapeDtypeStruct((T_CTX, D), BF16), kv_shape, kv_shape),
        grid=(N_CTX,),
        in_specs=in_specs,
        out_specs=(pl.BlockSpec((L_CTX, D), lambda n: (n, 0)), kv_spec, kv_spec),
        input_output_aliases=aliases,
        compiler_params=_cparams("arbitrary"),
        name="na_ctx",
    )(*args)


def _window_of_row(r):
    r0 = min(max(r - WIN_H // 2, 0), GRID_H - WIN_H)
    r0a = r0 & ~1
    rows = WIN_H if r0a == r0 else WIN_H + 2
    return r0, r0a, rows


def _na_bands():
    bands = []
    for rs in (range(0, GRID_H // 2), range(GRID_H // 2, GRID_H)):
        lo = min(_window_of_row(r)[1] for r in rs)
        hi = max(_window_of_row(r)[1] + _window_of_row(r)[2] for r in rs)
        bands.append((slice(rs[0] * GRID_W, (rs[-1] + 1) * GRID_W), slice(lo * GRID_W, hi * GRID_W)))
    return tuple(bands)


_NA_BANDS = _na_bands()


def _na_lat_kernel(q_ref, k_ref, v_ref, kc_ref, vc_ref, te_ref, to_ref, o_ref, *scratch):
    @pl.when((pl.program_id(0) == 0) & (pl.program_id(1) == 0))
    def _():
        scratch[2][...] = jnp.zeros_like(scratch[2])

    q2 = q_ref[...] * (HEAD_DIM ** -0.5)
    kb = k_ref[...].astype(BF16)
    vb = v_ref[...].astype(BF16)
    kc_t = jnp.concatenate([kc_ref[0], kc_ref[1]], axis=0).astype(BF16)
    vc_t = jnp.concatenate([vc_ref[0], vc_ref[1]], axis=0).astype(BF16)
    outs = []
    for e in range(2):
        s_scr, sc_scr, p_scr, pc_scr, li_scr = (ref.at[e] for ref in scratch)
        qm = jnp.where(_half_mask(e), q2, 0.0).astype(BF16)
        for q_rows, k_rows in _NA_BANDS:
            s_scr[q_rows, k_rows] = _dot_nt(qm[q_rows], kb[k_rows])
        sc_scr[...] = _dot(qm, kc_t)
        for r in range(GRID_H):
            r0, r0a, nrows = _window_of_row(r)
            width = nrows * GRID_W
            blk = r0a - r + WIN_H
            t_ref, t_off = (te_ref, blk * GRID_W) if r % 2 == 0 else (to_ref, (blk + 1) * GRID_W)
            rows = slice(r * GRID_W, (r + 1) * GRID_W)
            cols = slice(r0a * GRID_W, r0a * GRID_W + width)
            sl = s_scr[rows, cols] + t_ref[e, :, t_off:t_off + width]
            if nrows != WIN_H:
                lane = lax.broadcasted_iota(jnp.int32, (1, width), 1)
                sl = jnp.where((lane < GRID_W) | (lane >= width - GRID_W), NEG_INF, sl)
            sc = sc_scr[rows, :]
            m = jnp.maximum(jnp.max(sl, axis=-1, keepdims=True), jnp.max(sc, axis=-1, keepdims=True))
            p_loc = jnp.exp(sl - m)
            p_ctx = jnp.exp(sc - m)
            l = jnp.sum(p_loc, axis=-1, keepdims=True) + jnp.sum(p_ctx, axis=-1, keepdims=True)
            p_scr[rows, cols] = p_loc.astype(BF16)
            pc_scr[rows, :] = p_ctx.astype(BF16)
            li_scr[rows, :] = jnp.broadcast_to(1.0 / l, (GRID_W, 128))
        o_loc = jnp.concatenate([_dot(p_scr[q_rows, k_rows], vb[k_rows]) for q_rows, k_rows in _NA_BANDS], axis=0)
        outs.append((o_loc + _dot_nt(pc_scr[...], vc_t)) * li_scr[...])
    o_ref[...] = jnp.where(_half_mask(0), outs[0], outs[1]).astype(BF16)


def _na_lat_call(qkv, cache_k, cache_v, layer, t_even, t_odd):
    row0 = T_CTX // L_LAT
    ctx_spec = pl.BlockSpec((None, None, 2, HEAD_DIM, PAST), lambda b, hp: (b, layer, hp, 0, 0))
    t_spec = pl.BlockSpec((2, GRID_W, BIAS_BLOCKS * GRID_W), lambda b, hp: (hp, 0, 0))
    return pl.pallas_call(
        _na_lat_kernel,
        out_shape=jax.ShapeDtypeStruct((T_LAT, D), BF16),
        grid=(N_LAT, HEAD_PAIRS),
        in_specs=[
            pl.BlockSpec((L_LAT, 128), lambda b, hp: (row0 + b, hp)),
            pl.BlockSpec((L_LAT, 128), lambda b, hp: (row0 + b, HEAD_PAIRS + hp)),
            pl.BlockSpec((L_LAT, 128), lambda b, hp: (row0 + b, 2 * HEAD_PAIRS + hp)),
            ctx_spec, ctx_spec, t_spec, t_spec,
        ],
        out_specs=pl.BlockSpec((L_LAT, 128), lambda b, hp: (b, hp)),
        scratch_shapes=[
            pltpu.VMEM((2, L_LAT, L_LAT), F32),
            pltpu.VMEM((2, L_LAT, PAST), F32),
            pltpu.VMEM((2, L_LAT, L_LAT), BF16),
            pltpu.VMEM((2, L_LAT, PAST), BF16),
            pltpu.VMEM((2, L_LAT, 128), F32),
        ],
        compiler_params=_cparams("arbitrary", "arbitrary"),
        name="na_lat",
    )(qkv, qkv, qkv, cache_k, cache_v, t_even, t_odd)


def _ssd_kernel(*refs, seq, gs, has_init, has_final, has_prev):
    it = iter(refs)
    x_ref, b_ref, c_ref, dt_ref, bias_ref, nega_ref, d_ref = (next(it) for _ in range(7))
    init_ref = next(it) if has_init else None
    if has_prev:
        next(it)
    y_ref = next(it)
    fin_ref = next(it) if has_final else None
    sloc_scr, sin_scr = next(it), next(it)

    nc = seq // CHUNK
    g0 = pl.program_id(1) * gs
    rows = SSD_R * SSD_P

    src = lax.broadcasted_iota(jnp.int32, (128, 128), 0)
    dst = lax.broadcasted_iota(jnp.int32, (128, 128), 1)
    want = ((dst >> 2) & 1) * SSD_HEADS + (g0 + (dst >> 3)) * SSD_R + (dst & 3)
    sel = jnp.where((dst < 8 * gs) & (src == want), 1.0, 0.0).astype(BF16)
    dt_raw = sum(_dot(p, sel) for p in _split3(dt_ref[...]))
    u = dt_raw + bias_ref[...]
    dt = jnp.maximum(u, 0.0) + jnp.log1p(jnp.exp(-jnp.abs(u)))
    a = dt * nega_ref[...]

    li = lax.broadcasted_iota(jnp.int32, (CHUNK, CHUNK), 0)
    si = lax.broadcasted_iota(jnp.int32, (CHUNK, CHUNK), 1)
    tril = jnp.where(si <= li, 1.0, 0.0).astype(BF16)
    triu = jnp.where(si >= li, 1.0, 0.0).astype(BF16)
    pre, suf = [], []
    for c in range(nc):
        parts = _split3(a[c * CHUNK:(c + 1) * CHUNK])
        pre.append(sum(_dot(tril, p) for p in parts))
        suf.append(sum(_dot(triu, p) for p in parts))
    pre = jnp.concatenate(pre, axis=0)
    suf = jnp.concatenate(suf, axis=0)
    fwd_lane = (lax.broadcasted_iota(jnp.int32, (1, 128), 1) & SSD_R) == 0
    e1 = jnp.where(fwd_lane, pre, suf)
    e2 = jnp.where(fwd_lane, suf, pre) - a
    e1s = e1 * LOG2E
    e1s_t = e1s.T
    dt_t = dt.T
    ysc_t = jnp.exp(e1).T
    w2_t = jnp.exp(e2).T

    def expand(t, off, cols):
        return jnp.concatenate(
            [jnp.broadcast_to(t[off + h:off + h + 1, cols], (SSD_P, cols.stop - cols.start))
             for h in range(SSD_R)], axis=0)

    def chunk_decay(off, col):
        return expand(ysc_t, off, slice(col, col + 1))

    x_t = x_ref[...].T

    for gi in range(gs):
        fwd, bwd = 8 * gi, 8 * gi + SSD_R
        xg_t = x_t[gi * rows:(gi + 1) * rows]
        b_cols = slice(gi * SSD_N, (gi + 1) * SSD_N)
        y_cols = slice(gi * rows, (gi + 1) * rows)

        for c in range(nc):
            cols = slice(c * CHUNK, (c + 1) * CHUNK)
            xc = xg_t[:, cols]
            xw = jnp.concatenate([xc * (expand(dt_t, fwd, cols) * expand(w2_t, fwd, cols)),
                                  xc * (expand(dt_t, bwd, cols) * expand(w2_t, bwd, cols))], axis=0)
            sloc_scr[gi, c] = _dot(xw.astype(BF16), b_ref[cols, b_cols].astype(BF16))

        heads = slice(gi * SSD_R, (gi + 1) * SSD_R)
        s_f = init_ref[0, heads].reshape(rows, SSD_N) if has_init else None
        for c in range(nc):
            if s_f is not None:
                sin_scr[gi, c, 0:rows] = s_f
                s_f = s_f * chunk_decay(fwd, c * CHUNK + CHUNK - 1) + sloc_scr[gi, c, 0:rows]
            else:
                s_f = sloc_scr[gi, c, 0:rows]
        s_b = init_ref[1, heads].reshape(rows, SSD_N) if has_init else None
        for c in reversed(range(nc)):
            if s_b is not None:
                sin_scr[gi, c, rows:2 * rows] = s_b
                s_b = s_b * chunk_decay(bwd, c * CHUNK) + sloc_scr[gi, c, rows:2 * rows]
            else:
                s_b = sloc_scr[gi, c, rows:2 * rows]
        if has_final:
            fin_ref[0, heads] = s_f.reshape(SSD_R, SSD_P, SSD_N)
            fin_ref[1, heads] = s_b.reshape(SSD_R, SSD_P, SSD_N)

        for c in range(nc):
            cols = slice(c * CHUNK, (c + 1) * CHUNK)
            bc = b_ref[cols, b_cols].astype(BF16)
            cc = c_ref[cols, b_cols].astype(BF16)
            cb_t = _dot_nt(bc, cc)
            xc = xg_t[:, cols]
            xdf = xc * expand(dt_t, fwd, cols)
            xdb = xc * expand(dt_t, bwd, cols)
            entering = []
            if has_init or c > 0:
                entering.append((slice(0, rows), fwd))
            if has_init or c < nc - 1:
                entering.append((slice(rows, 2 * rows), bwd))
            off = None
            if entering:
                s_in = jnp.concatenate([sin_scr[gi, c, r] for r, _ in entering], axis=0)
                prod = _dot_nt(s_in.astype(BF16), cc)
                for k, (_, base) in enumerate(entering):
                    term = prod[k * rows:(k + 1) * rows] * expand(ysc_t, base, cols)
                    off = term if off is None else off + term
            y_heads = []
            for h in range(SSD_R):
                pf_col = e1s[cols, fwd + h:fwd + h + 1]
                rb_col = e1s[cols, bwd + h:bwd + h + 1]
                pf_row = e1s_t[fwd + h:fwd + h + 1, cols]
                rb_row = e1s_t[bwd + h:bwd + h + 1, cols]
                m_f = jnp.exp2(jnp.where(li <= si, pf_row - pf_col, NEG_INF)) * cb_t
                m_b = jnp.exp2(jnp.where(li >= si, rb_row - rb_col, NEG_INF)) * cb_t
                hs = slice(h * SSD_P, (h + 1) * SSD_P)
                lhs = jnp.concatenate([xdf[hs], xdb[hs]], axis=1).astype(BF16)
                rhs = jnp.concatenate([m_f, m_b], axis=0).astype(BF16)
                y_h = _dot(lhs, rhs)
                y_heads.append(y_h if off is None else y_h + off[hs])
            y_t = jnp.concatenate(y_heads, axis=0)
            y_ref[cols, y_cols] = y_t.T + x_ref[cols, y_cols] * d_ref[:, y_cols]


def _ssd_call(xbc, dt, dt_bias, a_log, d, state, fin_prev, layer, *, latent):
    seq = L_LAT if latent else L_CTX
    n_seq = N_LAT if latent else N_CTX
    gs = GS_LAT if latent else GS_CTX
    row0 = T_CTX // seq if latent else 0
    nc = seq // CHUNK
    xw, bw = gs * SSD_R * SSD_P, gs * SSD_N
    xcol0 = 0
    bcol0 = D_INNER // bw
    ccol0 = bcol0 + SSD_GROUPS // gs
    step_spec = lambda w: pl.BlockSpec((None, 1, w), lambda n, k: (k, 0, 0))
    in_specs = [
        pl.BlockSpec((seq, xw), lambda n, k: (row0 + n, xcol0 + k)),
        pl.BlockSpec((seq, bw), lambda n, k: (row0 + n, bcol0 + k)),
        pl.BlockSpec((seq, bw), lambda n, k: (row0 + n, ccol0 + k)),
        pl.BlockSpec((seq, 128), lambda n, k: (row0 + n, 0)),
        step_spec(128), step_spec(128), step_spec(xw),
    ]
    args = [xbc, xbc, xbc, dt, _step_lanes(dt_bias, gs), _step_lanes(-jnp.exp(a_log), gs),
            jnp.repeat(d, SSD_P).reshape(SSD_GROUPS // gs, 1, xw)]
    state_block = (None, None, 2, gs * SSD_R, SSD_P, SSD_N)
    aliases = {}
    if latent:
        in_specs.append(pl.BlockSpec(state_block, lambda n, k: (n, layer, 0, k, 0, 0)))
        args.append(state)
    elif fin_prev is not None:
        in_specs.append(pl.BlockSpec(memory_space=pl.ANY))
        args.append(fin_prev)
        aliases = {len(args) - 1: 1}
    y_shape = jax.ShapeDtypeStruct((n_seq * seq, D_INNER), F32)
    y_spec = pl.BlockSpec((seq, xw), lambda n, k: (n, k))
    if latent:
        out_shape, out_specs = y_shape, y_spec
    else:
        out_shape = (y_shape, jax.ShapeDtypeStruct((N_CTX, DEPTH // 2, 2, SSD_HEADS, SSD_P, SSD_N), F32))
        out_specs = (y_spec, pl.BlockSpec(state_block, lambda n, k: (n, layer, 0, k, 0, 0)))
    scratch = pltpu.VMEM((gs, nc, 2 * SSD_R * SSD_P, SSD_N), F32)
    return pl.pallas_call(
        functools.partial(_ssd_kernel, seq=seq, gs=gs, has_init=latent, has_final=not latent,
                          has_prev=bool(aliases)),
        out_shape=out_shape,
        grid=(n_seq, SSD_GROUPS // gs),
        in_specs=in_specs,
        out_specs=out_specs,
        input_output_aliases=aliases,
        scratch_shapes=[scratch, scratch],
        compiler_params=_cparams("arbitrary", "arbitrary"),
        name="ssd_scan_lat" if latent else "ssd_scan_ctx",
    )(*args)


def _step_lanes(p, gs):
    q = p.reshape(2, SSD_GROUPS, SSD_R).transpose(1, 0, 2).reshape(SSD_GROUPS // gs, 8 * gs)
    return jnp.pad(q, ((0, 0), (0, 128 - 8 * gs))).reshape(SSD_GROUPS // gs, 1, 128)


def kernel(x_prompt, x_sample, cache_k, cache_v, state_ssm, c, c_ctx, ada_w, ada_b, norm_mix_g, norm_ffn_g,
           ffn_w_gate_up, ffn_w_down, na_w_qkv, na_w_o, na_rpb, ssd_w_in, ssd_conv_w, ssd_conv_b,
           ssd_dt_bias, ssd_a_log, ssd_d, ssd_norm_g, ssd_w_out, final_norm_g):
    cond = jnp.zeros((N_COND, D), F32).at[0].set(c_ctx).at[1:1 + N_LAT].set(c)
    mods = _ada_call(cond, ada_w, ada_b).reshape(DEPTH, N_COND, 6, 1, D)
    x = (x_prompt.reshape(T_CTX, D), x_sample.reshape(T_LAT, D))
    final_g = final_norm_g.reshape(1, D)
    w_qkv, w_o = na_w_qkv.astype(BF16), na_w_o.astype(BF16)
    w_in, w_out = ssd_w_in.astype(BF16), ssd_w_out.astype(BF16)
    w_gu, w_d = ffn_w_gate_up.astype(BF16), ffn_w_down.astype(BF16)
    conv_b = ssd_conv_b.reshape(DEPTH // 2, 1, CONV_DIM)
    cache_k_t, cache_v_t = cache_k.swapaxes(-1, -2), cache_v.swapaxes(-1, -2)

    new_kv, new_s = None, None
    for i in range(DEPTH):
        j = i // 2
        m = [mods[i, :, t] for t in range(6)]
        g_mix = norm_mix_g[i].reshape(1, D)
        if i % 2 == 0:
            qkv = _qkv_call(x, g_mix, m[0], m[1], w_qkv, j)
            o_ctx, *new_kv = _na_ctx_call(qkv, new_kv, j)
            t_even, t_odd = _bias_call(na_rpb[j])
            o_lat = _na_lat_call(qkv, cache_k_t, cache_v_t, j, t_even, t_odd)
            x = _oproj_call(o_ctx, o_lat, x, m[2], w_o, j)
        else:
            w_dt = jnp.pad(w_in[j, :, ZX_DIM:], ((0, 0), (0, 128 - 2 * SSD_HEADS)))
            z, xbc, dt = _inproj_call(x, g_mix, m[0], m[1], w_in, w_dt, ssd_conv_w, conv_b, j)
            scan_args = (xbc, dt, ssd_dt_bias[j], ssd_a_log[j], ssd_d[j])
            y_ctx, new_s = _ssd_call(*scan_args, None, new_s, j, latent=False)
            y_lat = _ssd_call(*scan_args, state_ssm, None, j, latent=True)
            x = _ssd_out_call(y_ctx, y_lat, z, x, m[2], ssd_norm_g[j].reshape(1, D_INNER), w_out, j)
        x = _ffn_call(x, norm_ffn_g[i].reshape(1, D), m[3], m[4], m[5], w_gu, w_d, final_g, i,
                      final_norm=(i == DEPTH - 1), split_out=(i == DEPTH - 1))

    y_prompt, y_sample = x
    return (y_prompt.reshape(N_CTX, L_CTX, D), y_sample.reshape(N_LAT, L_LAT, D), new_kv[0], new_kv[1], new_s)
```

```python
import functools

import jax
import jax.numpy as jnp
from jax import lax
from jax.experimental import pallas as pl
from jax.experimental.pallas import tpu as pltpu

F32 = jnp.float32
BF16 = jnp.bfloat16

D = 1024
N_CTX, L_CTX = 32, 256
N_LAT, L_LAT = 4, 1024
T_CTX = N_CTX * L_CTX
T_LAT = N_LAT * L_LAT
T_ALL = T_CTX + T_LAT
DEPTH = 4
N_COND = 8
PAST = 512
GRID_W = 64
GRID_H = L_LAT // GRID_W
HEADS = 16
HEAD_DIM = 64
HEAD_PAIRS = HEADS // 2
WIN_H, WIN_W = 8, 16
RPB_H, RPB_W = 2 * WIN_H - 1, 2 * WIN_W - 1
D_INNER = 2048
SSD_HEADS = 32
SSD_P = 64
SSD_GROUPS = 8
SSD_R = SSD_HEADS // SSD_GROUPS
SSD_N = 128
CHUNK = 128
CONV_DIM = D_INNER + 2 * SSD_GROUPS * SSD_N
ZX_DIM = D_INNER + CONV_DIM
D_FF = 2816
EPS = 1e-6
NEG_INF = float("-inf")
LOG2E = 1.4426950408889634

TM = 1024
TM_FFN = 1024
MXU_DIM = 256
FF_CUTS = (0, (D_FF // MXU_DIM + 1) // 2 * MXU_DIM, D_FF)
TN_QKV = 1024
TN_IN = 1024
MM_COLS = 512
TM_OUT = 512
GS_CTX = 8
GS_LAT = 4
BIAS_BLOCKS = 18
VMEM_LIMIT = 56 * 1024 * 1024


def _cparams(*sem):
    return pltpu.CompilerParams(dimension_semantics=sem, vmem_limit_bytes=VMEM_LIMIT)


def _cond_of_tile(i, tm):
    return jnp.maximum((i * tm) // L_LAT - (T_CTX // L_LAT - 1), 0)


def _silu(x):
    return x * (0.5 * jnp.tanh(0.5 * x) + 0.5)


def _rms(x, g):
    return x * lax.rsqrt(jnp.mean(x * x, axis=-1, keepdims=True) + EPS) * g


def _norm_mod(x, g, shift, scale):
    return _rms(x, g) * (1.0 + scale) + shift


def _dot(a, b):
    return jnp.dot(a, b, preferred_element_type=F32)


def _dot_nt(a, b):
    return lax.dot_general(a, b, (((1,), (1,)), ((), ())), preferred_element_type=F32)


def _split3(x):
    hi = x.astype(BF16)
    r = x - hi.astype(F32)
    mid = r.astype(BF16)
    lo = (r - mid.astype(F32)).astype(BF16)
    return hi, mid, lo


def _ada_kernel(c_ref, w_ref, b_ref, o_ref):
    s = _silu(c_ref[...]).astype(BF16)
    o_ref[...] = _dot(s, w_ref[...].astype(BF16)) + b_ref[...]


def _ada_call(cond, ada_w, ada_b):
    tn = 1536
    return pl.pallas_call(
        _ada_kernel,
        out_shape=jax.ShapeDtypeStruct((DEPTH, N_COND, 6 * D), F32),
        grid=(DEPTH, 6 * D // tn),
        in_specs=[
            pl.BlockSpec((N_COND, D), lambda l, j: (0, 0)),
            pl.BlockSpec((None, D, tn), lambda l, j: (l, 0, j)),
            pl.BlockSpec((None, 1, tn), lambda l, j: (l, 0, j)),
        ],
        out_specs=pl.BlockSpec((None, N_COND, tn), lambda l, j: (l, 0, j)),
        compiler_params=_cparams("arbitrary", "arbitrary"),
        name="ada",
    )(cond, ada_w, ada_b.reshape(DEPTH, 1, 6 * D))


def _mod_spec(tm):
    return pl.BlockSpec((None, 1, D), lambda i, *_: (_cond_of_tile(i, tm), 0, 0))


def _stream_specs(streams, tm, width):
    n_ctx = T_CTX // tm
    lat0 = 0 if isinstance(streams, tuple) else n_ctx
    arrays = streams if isinstance(streams, tuple) else (streams, streams)
    specs = [pl.BlockSpec((tm, width), lambda i, *_: (jnp.minimum(i, n_ctx - 1), 0)),
             pl.BlockSpec((tm, width), lambda i, *_: (lat0 + jnp.maximum(i - n_ctx, 0), 0))]
    return list(arrays), specs


def _on_stream(tm, fn, ctx_ref, lat_ref, extra=True):
    is_ctx = pl.program_id(0) < T_CTX // tm
    pl.when(is_ctx & extra)(lambda: fn(ctx_ref))
    pl.when(jnp.logical_not(is_ctx) & extra)(lambda: fn(lat_ref))


def _qkv_kernel(xc_ref, xl_ref, g_ref, sh_ref, sc_ref, w_ref, o_ref, h_scr, w_scr):
    @pl.when((pl.program_id(0) == 0) & (pl.program_id(1) == 0))
    def _():
        for t in range(w_scr.shape[0]):
            w_scr[t] = w_ref[:, t * TN_QKV:(t + 1) * TN_QKV].astype(BF16)

    def prologue(x_ref):
        h_scr[...] = _norm_mod(x_ref[...], g_ref[...], sh_ref[...], sc_ref[...]).astype(BF16)

    _on_stream(TM, prologue, xc_ref, xl_ref, pl.program_id(1) == 0)
    o_ref[...] = _dot(h_scr[...], w_scr[pl.program_id(1)])


def _qkv_call(x, g, shift, scale, w_all_f32, layer):
    n = w_all_f32.shape[2]
    x_args, x_specs = _stream_specs(x, TM, D)
    return pl.pallas_call(
        _qkv_kernel,
        out_shape=jax.ShapeDtypeStruct((T_ALL, n), F32),
        grid=(T_ALL // TM, n // TN_QKV),
        in_specs=x_specs + [
            pl.BlockSpec((1, D), lambda i, j: (0, 0)),
            _mod_spec(TM),
            _mod_spec(TM),
            pl.BlockSpec((None, D, n), lambda i, j: (layer, 0, 0), pipeline_mode=pl.Buffered(1)),
        ],
        out_specs=pl.BlockSpec((TM, TN_QKV), lambda i, j: (i, j)),
        scratch_shapes=[pltpu.VMEM((TM, D), BF16), pltpu.VMEM((n // TN_QKV, D, TN_QKV), BF16)],
        compiler_params=_cparams("arbitrary", "arbitrary"),
        name="na_qkv",
    )(*x_args, g, shift, scale, w_all_f32)


def _inproj_kernel(x_ref, g_ref, sh_ref, sc_ref, w_ref, wdt_ref, cw_ref, cb_ref, z_ref, xbc_ref, dt_ref,
                   h_scr, c_scr):
    i = pl.program_id(0)
    s = pl.program_id(1)
    n_z = D_INNER // TN_IN

    @pl.when((i == 0) & (s == 0))
    def _():
        c_scr[0:8, :] = jnp.zeros((8, TN_IN), F32)
        c_scr[TM + 8:TM + 16, :] = jnp.zeros((8, TN_IN), F32)

    @pl.when(s == 0)
    def _():
        h = _norm_mod(x_ref[...], g_ref[...], sh_ref[...], sc_ref[...]).astype(BF16)
        h_scr[...] = h
        dt_ref[...] = _dot(h, wdt_ref[...])

    @pl.when(s < n_z)
    def _():
        z_ref[...] = _dot(h_scr[...], w_ref[...])

    @pl.when(s >= n_z)
    def _():
        seq = jnp.where(i < T_CTX // TM, L_CTX, L_LAT)
        pos = lax.broadcasted_iota(jnp.int32, (TM, 1), 0) & (seq - 1)
        for m0 in range(0, TN_IN, MM_COLS):
            cols = slice(m0, m0 + MM_COLS)
            acc = _dot(h_scr[...], w_ref[:, cols])
            c_scr[8:TM + 8, cols] = acc
            prev = jnp.where(pos == 0, 0.0, c_scr[7:TM + 7, cols])
            nxt = jnp.where(pos == seq - 1, 0.0, c_scr[9:TM + 9, cols])
            y = cw_ref[0:1, cols] * prev + cw_ref[1:2, cols] * acc + cw_ref[2:3, cols] * nxt + cb_ref[:, cols]
            xbc_ref[:, cols] = _silu(y)


def _inproj_call(x, g, shift, scale, w_all, w_dt, conv_w_all, conv_b_all, layer):
    n_z = D_INNER // TN_IN
    n_steps = ZX_DIM // TN_IN
    conv_tile = lambda s: jnp.maximum(s - n_z, 0)
    return pl.pallas_call(
        _inproj_kernel,
        out_shape=(jax.ShapeDtypeStruct((T_ALL, D_INNER), F32),
                   jax.ShapeDtypeStruct((T_ALL, CONV_DIM), F32),
                   jax.ShapeDtypeStruct((T_ALL, 128), F32)),
        grid=(T_ALL // TM, n_steps),
        in_specs=[
            pl.BlockSpec((TM, D), lambda i, s: (i, 0)),
            pl.BlockSpec((1, D), lambda i, s: (0, 0)),
            _mod_spec(TM),
            _mod_spec(TM),
            pl.BlockSpec((None, D, TN_IN), lambda i, s: (layer, 0, s)),
            pl.BlockSpec((D, 128), lambda i, s: (0, 0)),
            pl.BlockSpec((None, 3, TN_IN), lambda i, s: (layer, 0, conv_tile(s))),
            pl.BlockSpec((None, 1, TN_IN), lambda i, s: (layer, 0, conv_tile(s))),
        ],
        out_specs=(pl.BlockSpec((TM, TN_IN), lambda i, s: (i, jnp.minimum(s, n_z - 1))),
                   pl.BlockSpec((TM, TN_IN), lambda i, s: (i, conv_tile(s))),
                   pl.BlockSpec((TM, 128), lambda i, s: (i, 0))),
        scratch_shapes=[pltpu.VMEM((TM, D), BF16), pltpu.VMEM((TM + 16, TN_IN), F32)],
        compiler_params=_cparams("arbitrary", "arbitrary"),
        name="ssd_inproj",
    )(x, g, shift, scale, w_all, w_dt, conv_w_all, conv_b_all)


def _oproj_kernel(ac_ref, al_ref, xc_ref, xl_ref, gt_ref, w_ref, o_ref):
    def run(refs):
        a_ref, x_ref = refs
        o_ref[...] = x_ref[...] + gt_ref[...] * _dot(a_ref[...], w_ref[...])

    _on_stream(TM_OUT, run, (ac_ref, xc_ref), (al_ref, xl_ref))


def _oproj_call(a_ctx, a_lat, x, gate, w_all, layer):
    a_args, a_specs = _stream_specs((a_ctx, a_lat), TM_OUT, D)
    x_args, x_specs = _stream_specs(x, TM_OUT, D)
    return pl.pallas_call(
        _oproj_kernel,
        out_shape=jax.ShapeDtypeStruct((T_ALL, D), F32),
        grid=(T_ALL // TM_OUT,),
        in_specs=a_specs + x_specs + [
            _mod_spec(TM_OUT),
            pl.BlockSpec((None, D, D), lambda i: (layer, 0, 0)),
        ],
        out_specs=pl.BlockSpec((TM_OUT, D), lambda i: (i, 0)),
        compiler_params=_cparams("arbitrary"),
        name="na_oproj",
    )(*a_args, *x_args, gate, w_all)


def _ssd_out_kernel(yc_ref, yl_ref, z_ref, x_ref, gt_ref, ng_ref, w_ref, o_ref):
    def run(y_ref):
        v = y_ref[...] * _silu(z_ref[...])
        v = _rms(v, ng_ref[...]).astype(BF16)
        o_ref[...] = x_ref[...] + gt_ref[...] * _dot(v, w_ref[...])

    _on_stream(TM_OUT, run, yc_ref, yl_ref)


def _ssd_out_call(y_ctx, y_lat, z, x, gate, norm_g, w_all, layer):
    y_args, y_specs = _stream_specs((y_ctx, y_lat), TM_OUT, D_INNER)
    return pl.pallas_call(
        _ssd_out_kernel,
        out_shape=jax.ShapeDtypeStruct((T_ALL, D), F32),
        grid=(T_ALL // TM_OUT,),
        in_specs=y_specs + [
            pl.BlockSpec((TM_OUT, D_INNER), lambda i: (i, 0)),
            pl.BlockSpec((TM_OUT, D), lambda i: (i, 0)),
            _mod_spec(TM_OUT),
            pl.BlockSpec((1, D_INNER), lambda i: (0, 0)),
            pl.BlockSpec((None, D_INNER, D), lambda i: (layer, 0, 0)),
        ],
        out_specs=pl.BlockSpec((TM_OUT, D), lambda i: (i, 0)),
        compiler_params=_cparams("arbitrary"),
        name="ssd_out",
    )(*y_args, z, x, gate, norm_g, w_all)


def _ffn_kernel(x_ref, g_ref, sh_ref, sc_ref, gt_ref, wgu_ref, wd_ref, fg_ref, *o_refs, final_norm):
    half = TM_FFN // 2
    for r in range(2):
        rows = slice(r * half, (r + 1) * half)
        h = _norm_mod(x_ref[rows, :], g_ref[...], sh_ref[...], sc_ref[...]).astype(BF16)
        acc = None
        for lo, hi in zip(FF_CUTS[:-1], FF_CUTS[1:]):
            gate_cols = slice(lo, hi)
            up_cols = slice(D_FF + lo, D_FF + hi)
            act = (_silu(_dot(h, wgu_ref[:, gate_cols])) * _dot(h, wgu_ref[:, up_cols])).astype(BF16)
            part = _dot(act, wd_ref[gate_cols, :])
            acc = part if acc is None else acc + part
        y = x_ref[rows, :] + gt_ref[...] * acc
        if final_norm:
            y = _rms(y, fg_ref[...])
        if len(o_refs) == 1:
            o_refs[0][rows, :] = y
        else:
            def store(o_ref, y=y, rows=rows):
                o_ref[rows, :] = y

            _on_stream(TM_FFN, store, *o_refs)


def _ffn_call(x, g, shift, scale, gate, w_gu_all, w_d_all, final_g, layer, final_norm, split_out=False):
    resident = dict(pipeline_mode=pl.Buffered(1))
    n_ctx = T_CTX // TM_FFN
    if split_out:
        out_shape = (jax.ShapeDtypeStruct((T_CTX, D), F32), jax.ShapeDtypeStruct((T_LAT, D), F32))
        out_specs = (pl.BlockSpec((TM_FFN, D), lambda i: (jnp.minimum(i, n_ctx - 1), 0)),
                     pl.BlockSpec((TM_FFN, D), lambda i: (jnp.maximum(i - n_ctx, 0), 0)))
    else:
        out_shape = jax.ShapeDtypeStruct((T_ALL, D), F32)
        out_specs = pl.BlockSpec((TM_FFN, D), lambda i: (i, 0))
    return pl.pallas_call(
        functools.partial(_ffn_kernel, final_norm=final_norm),
        out_shape=out_shape,
        grid=(T_ALL // TM_FFN,),
        in_specs=[
            pl.BlockSpec((TM_FFN, D), lambda i: (i, 0)),
            pl.BlockSpec((1, D), lambda i: (0, 0)),
            _mod_spec(TM_FFN),
            _mod_spec(TM_FFN),
            _mod_spec(TM_FFN),
            pl.BlockSpec((None, D, 2 * D_FF), lambda i: (layer, 0, 0), **resident),
            pl.BlockSpec((None, D_FF, D), lambda i: (layer, 0, 0), **resident),
            pl.BlockSpec((1, D), lambda i: (0, 0)),
        ],
        out_specs=out_specs,
        compiler_params=_cparams("arbitrary"),
        name="ffn",
    )(x, g, shift, scale, gate, w_gu_all, w_d_all, final_g)


def _bias_kernel(rpb_ref, te_ref, to_ref):
    qc = lax.broadcasted_iota(jnp.int32, (GRID_W, 128), 0)
    lane = lax.broadcasted_iota(jnp.int32, (GRID_W, 128), 1)
    kc = lane & (GRID_W - 1)
    upper = lane >= GRID_W
    start = jnp.clip(qc - WIN_W // 2, 0, GRID_W - WIN_W)
    in_win = (kc >= start) & (kc < start + WIN_W)

    def block(dr, lane0):
        base = jnp.broadcast_to(rpb_ref[dr:dr + 1, :], (GRID_W, 128))
        return pltpu.roll(base, (lane0 - (WIN_W - 1)) % 128, 1, stride=1, stride_axis=0)

    def pair_tile(dr_lo, dr_hi):
        def ok(dr):
            return 0 <= dr < RPB_H

        if ok(dr_lo) and ok(dr_hi):
            return jnp.where(in_win, jnp.where(upper, block(dr_hi, GRID_W), block(dr_lo, 0)), NEG_INF)
        if ok(dr_lo):
            return jnp.where(in_win & jnp.logical_not(upper), block(dr_lo, 0), NEG_INF)
        if ok(dr_hi):
            return jnp.where(in_win & upper, block(dr_hi, GRID_W), NEG_INF)
        return jnp.full((GRID_W, 128), NEG_INF, F32)

    for t in range(BIAS_BLOCKS // 2):
        te_ref[:, t * 128:(t + 1) * 128] = pair_tile(2 * t - 1, 2 * t)
        to_ref[:, t * 128:(t + 1) * 128] = pair_tile(2 * t - 2, 2 * t - 1)


def _bias_call(rpb):
    shape = jax.ShapeDtypeStruct((HEADS, GRID_W, BIAS_BLOCKS * GRID_W), F32)
    spec = pl.BlockSpec((None, GRID_W, BIAS_BLOCKS * GRID_W), lambda h: (h, 0, 0))
    rows = jnp.pad(rpb, ((0, 0), (0, 16 - RPB_H), (0, 128 - RPB_W)))
    return pl.pallas_call(
        _bias_kernel,
        out_shape=(shape, shape),
        grid=(HEADS,),
        in_specs=[pl.BlockSpec((None, 16, 128), lambda h: (h, 0, 0))],
        out_specs=(spec, spec),
        compiler_params=_cparams("arbitrary"),
        name="na_bias",
    )(rows)


def _half_mask(e):
    lane = lax.broadcasted_iota(jnp.int32, (1, 128), 1)
    return (lane >= HEAD_DIM) if e else (lane < HEAD_DIM)


def _na_ctx_kernel(qkv_ref, *refs):
    o_ref, ko_ref, vo_ref = refs[-3:]
    scale = HEAD_DIM ** -0.5
    for hp in range(HEAD_PAIRS):
        q2 = qkv_ref[:, hp * 128:(hp + 1) * 128] * scale
        k2 = qkv_ref[:, D + hp * 128:D + (hp + 1) * 128]
        v2 = qkv_ref[:, 2 * D + hp * 128:2 * D + (hp + 1) * 128]
        for e in range(2):
            ko_ref[2 * hp + e] = k2[:, e * HEAD_DIM:(e + 1) * HEAD_DIM]
            vo_ref[2 * hp + e] = v2[:, e * HEAD_DIM:(e + 1) * HEAD_DIM]
        kb = k2.astype(BF16)
        vb = v2.astype(BF16)
        outs = []
        for e in range(2):
            qm = jnp.where(_half_mask(e), q2, 0.0).astype(BF16)
            s = _dot_nt(qm, kb)
            p = jnp.exp(s - jnp.max(s, axis=-1, keepdims=True))
            l = jnp.sum(p, axis=-1, keepdims=True)
            outs.append(_dot(p.astype(BF16), vb) / l)
        o_ref[:, hp * 128:(hp + 1) * 128] = jnp.where(_half_mask(0), outs[0], outs[1]).astype(BF16)


def _na_ctx_call(qkv, kv_prev, layer):
    kv_shape = jax.ShapeDtypeStruct((N_CTX, (DEPTH + 1) // 2, HEADS, L_CTX, HEAD_DIM), F32)
    kv_spec = pl.BlockSpec((None, None, HEADS, L_CTX, HEAD_DIM), lambda n: (n, layer, 0, 0, 0))
    in_specs = [pl.BlockSpec((L_CTX, 3 * D), lambda n: (n, 0))]
    args = [qkv]
    aliases = {}
    if kv_prev is not None:
        in_specs += [pl.BlockSpec(memory_space=pl.ANY)] * 2
        args += list(kv_prev)
        aliases = {1: 1, 2: 2}
    return pl.pallas_call(
        _na_ctx_kernel,
        out_shape=(jax.ShapeDtypeStruct((T_CTX, D), BF16), kv_shape, kv_shape),
        grid=(N_CTX,),
        in_specs=in_specs,
        out_specs=(pl.BlockSpec((L_CTX, D), lambda n: (n, 0)), kv_spec, kv_spec),
        input_output_aliases=aliases,
        compiler_params=_cparams("arbitrary"),
        name="na_ctx",
    )(*args)


def _window_of_row(r):
    r0 = min(max(r - WIN_H // 2, 0), GRID_H - WIN_H)
    r0a = r0 & ~1
    rows = WIN_H if r0a == r0 else WIN_H + 2
    return r0, r0a, rows


def _na_bands():
    bands = []
    for rs in (range(0, GRID_H // 2), range(GRID_H // 2, GRID_H)):
        lo = min(_window_of_row(r)[1] for r in rs)
        hi = max(_window_of_row(r)[1] + _window_of_row(r)[2] for r in rs)
        bands.append((slice(rs[0] * GRID_W, (rs[-1] + 1) * GRID_W), slice(lo * GRID_W, hi * GRID_W)))
    return tuple(bands)


_NA_BANDS = _na_bands()


def _na_lat_kernel(q_ref, k_ref, v_ref, kc_ref, vc_ref, te_ref, to_ref, o_ref, *scratch):
    @pl.when((pl.program_id(0) == 0) & (pl.program_id(1) == 0))
    def _():
        scratch[2][...] = jnp.zeros_like(scratch[2])

    q2 = q_ref[...] * (HEAD_DIM ** -0.5)
    kb = k_ref[...].astype(BF16)
    vb = v_ref[...].astype(BF16)
    kc_t = jnp.concatenate([kc_ref[0], kc_ref[1]], axis=0).astype(BF16)
    vc_t = jnp.concatenate([vc_ref[0], vc_ref[1]], axis=0).astype(BF16)
    outs = []
    for e in range(2):
        s_scr, sc_scr, p_scr, pc_scr, li_scr = (ref.at[e] for ref in scratch)
        qm = jnp.where(_half_mask(e), q2, 0.0).astype(BF16)
        for q_rows, k_rows in _NA_BANDS:
            s_scr[q_rows, k_rows] = _dot_nt(qm[q_rows], kb[k_rows])
        sc_scr[...] = _dot(qm, kc_t)
        for r in range(GRID_H):
            r0, r0a, nrows = _window_of_row(r)
            width = nrows * GRID_W
            blk = r0a - r + WIN_H
            t_ref, t_off = (te_ref, blk * GRID_W) if r % 2 == 0 else (to_ref, (blk + 1) * GRID_W)
            rows = slice(r * GRID_W, (r + 1) * GRID_W)
            cols = slice(r0a * GRID_W, r0a * GRID_W + width)
            sl = s_scr[rows, cols] + t_ref[e, :, t_off:t_off + width]
            if nrows != WIN_H:
                lane = lax.broadcasted_iota(jnp.int32, (1, width), 1)
                sl = jnp.where((lane < GRID_W) | (lane >= width - GRID_W), NEG_INF, sl)
            sc = sc_scr[rows, :]
            m = jnp.maximum(jnp.max(sl, axis=-1, keepdims=True), jnp.max(sc, axis=-1, keepdims=True))
            p_loc = jnp.exp(sl - m)
            p_ctx = jnp.exp(sc - m)
            l = jnp.sum(p_loc, axis=-1, keepdims=True) + jnp.sum(p_ctx, axis=-1, keepdims=True)
            p_scr[rows, cols] = p_loc.astype(BF16)
            pc_scr[rows, :] = p_ctx.astype(BF16)
            li_scr[rows, :] = jnp.broadcast_to(1.0 / l, (GRID_W, 128))
        o_loc = jnp.concatenate([_dot(p_scr[q_rows, k_rows], vb[k_rows]) for q_rows, k_rows in _NA_BANDS], axis=0)
        outs.append((o_loc + _dot_nt(pc_scr[...], vc_t)) * li_scr[...])
    o_ref[...] = jnp.where(_half_mask(0), outs[0], outs[1]).astype(BF16)


def _na_lat_call(qkv, cache_k, cache_v, layer, t_even, t_odd):
    row0 = T_CTX // L_LAT
    ctx_spec = pl.BlockSpec((None, None, 2, HEAD_DIM, PAST), lambda b, hp: (b, layer, hp, 0, 0))
    t_spec = pl.BlockSpec((2, GRID_W, BIAS_BLOCKS * GRID_W), lambda b, hp: (hp, 0, 0))
    return pl.pallas_call(
        _na_lat_kernel,
        out_shape=jax.ShapeDtypeStruct((T_LAT, D), BF16),
        grid=(N_LAT, HEAD_PAIRS),
        in_specs=[
            pl.BlockSpec((L_LAT, 128), lambda b, hp: (row0 + b, hp)),
            pl.BlockSpec((L_LAT, 128), lambda b, hp: (row0 + b, HEAD_PAIRS + hp)),
            pl.BlockSpec((L_LAT, 128), lambda b, hp: (row0 + b, 2 * HEAD_PAIRS + hp)),
            ctx_spec, ctx_spec, t_spec, t_spec,
        ],
        out_specs=pl.BlockSpec((L_LAT, 128), lambda b, hp: (b, hp)),
        scratch_shapes=[
            pltpu.VMEM((2, L_LAT, L_LAT), F32),
            pltpu.VMEM((2, L_LAT, PAST), F32),
            pltpu.VMEM((2, L_LAT, L_LAT), BF16),
            pltpu.VMEM((2, L_LAT, PAST), BF16),
            pltpu.VMEM((2, L_LAT, 128), F32),
        ],
        compiler_params=_cparams("arbitrary", "arbitrary"),
        name="na_lat",
    )(qkv, qkv, qkv, cache_k, cache_v, t_even, t_odd)


def _ssd_kernel(*refs, seq, gs, has_init, has_final, has_prev):
    it = iter(refs)
    x_ref, b_ref, c_ref, dt_ref, bias_ref, nega_ref, d_ref = (next(it) for _ in range(7))
    init_ref = next(it) if has_init else None
    if has_prev:
        next(it)
    y_ref = next(it)
    fin_ref = next(it) if has_final else None
    sloc_scr, sin_scr = next(it), next(it)

    nc = seq // CHUNK
    g0 = pl.program_id(1) * gs
    rows = SSD_R * SSD_P

    src = lax.broadcasted_iota(jnp.int32, (128, 128), 0)
    dst = lax.broadcasted_iota(jnp.int32, (128, 128), 1)
    want = ((dst >> 2) & 1) * SSD_HEADS + (g0 + (dst >> 3)) * SSD_R + (dst & 3)
    sel = jnp.where((dst < 8 * gs) & (src == want), 1.0, 0.0).astype(BF16)
    dt_raw = sum(_dot(p, sel) for p in _split3(dt_ref[...]))
    u = dt_raw + bias_ref[...]
    dt = jnp.maximum(u, 0.0) + jnp.log1p(jnp.exp(-jnp.abs(u)))
    a = dt * nega_ref[...]

    li = lax.broadcasted_iota(jnp.int32, (CHUNK, CHUNK), 0)
    si = lax.broadcasted_iota(jnp.int32, (CHUNK, CHUNK), 1)
    tril = jnp.where(si <= li, 1.0, 0.0).astype(BF16)
    triu = jnp.where(si >= li, 1.0, 0.0).astype(BF16)
    pre, suf = [], []
    for c in range(nc):
        parts = _split3(a[c * CHUNK:(c + 1) * CHUNK])
        pre.append(sum(_dot(tril, p) for p in parts))
        suf.append(sum(_dot(triu, p) for p in parts))
    pre = jnp.concatenate(pre, axis=0)
    suf = jnp.concatenate(suf, axis=0)
    fwd_lane = (lax.broadcasted_iota(jnp.int32, (1, 128), 1) & SSD_R) == 0
    e1 = jnp.where(fwd_lane, pre, suf)
    e2 = jnp.where(fwd_lane, suf, pre) - a
    e1s = e1 * LOG2E
    e1s_t = e1s.T
    dt_t = dt.T
    ysc_t = jnp.exp(e1).T
    w2_t = jnp.exp(e2).T

    def expand(t, off, cols):
        return jnp.concatenate(
            [jnp.broadcast_to(t[off + h:off + h + 1, cols], (SSD_P, cols.stop - cols.start))
             for h in range(SSD_R)], axis=0)

    def chunk_decay(off, col):
        return expand(ysc_t, off, slice(col, col + 1))

    x_t = x_ref[...].T

    for gi in range(gs):
        fwd, bwd = 8 * gi, 8 * gi + SSD_R
        xg_t = x_t[gi * rows:(gi + 1) * rows]
        b_cols = slice(gi * SSD_N, (gi + 1) * SSD_N)
        y_cols = slice(gi * rows, (gi + 1) * rows)

        for c in range(nc):
            cols = slice(c * CHUNK, (c + 1) * CHUNK)
            xc = xg_t[:, cols]
            xw = jnp.concatenate([xc * (expand(dt_t, fwd, cols) * expand(w2_t, fwd, cols)),
                                  xc * (expand(dt_t, bwd, cols) * expand(w2_t, bwd, cols))], axis=0)
            sloc_scr[gi, c] = _dot(xw.astype(BF16), b_ref[cols, b_cols].astype(BF16))

        heads = slice(gi * SSD_R, (gi + 1) * SSD_R)
        s_f = init_ref[0, heads].reshape(rows, SSD_N) if has_init else None
        for c in range(nc):
            if s_f is not None:
                sin_scr[gi, c, 0:rows] = s_f
                s_f = s_f * chunk_decay(fwd, c * CHUNK + CHUNK - 1) + sloc_scr[gi, c, 0:rows]
            else:
                s_f = sloc_scr[gi, c, 0:rows]
        s_b = init_ref[1, heads].reshape(rows, SSD_N) if has_init else None
        for c in reversed(range(nc)):
            if s_b is not None:
                sin_scr[gi, c, rows:2 * rows] = s_b
                s_b = s_b * chunk_decay(bwd, c * CHUNK) + sloc_scr[gi, c, rows:2 * rows]
            else:
                s_b = sloc_scr[gi, c, rows:2 * rows]
        if has_final:
            fin_ref[0, heads] = s_f.reshape(SSD_R, SSD_P, SSD_N)
            fin_ref[1, heads] = s_b.reshape(SSD_R, SSD_P, SSD_N)

        for c in range(nc):
            cols = slice(c * CHUNK, (c + 1) * CHUNK)
            bc = b_ref[cols, b_cols].astype(BF16)
            cc = c_ref[cols, b_cols].astype(BF16)
            cb_t = _dot_nt(bc, cc)
            xc = xg_t[:, cols]
            xdf = xc * expand(dt_t, fwd, cols)
            xdb = xc * expand(dt_t, bwd, cols)
            entering = []
            if has_init or c > 0:
                entering.append((slice(0, rows), fwd))
            if has_init or c < nc - 1:
                entering.append((slice(rows, 2 * rows), bwd))
            off = None
            if entering:
                s_in = jnp.concatenate([sin_scr[gi, c, r] for r, _ in entering], axis=0)
                prod = _dot_nt(s_in.astype(BF16), cc)
                for k, (_, base) in enumerate(entering):
                    term = prod[k * rows:(k + 1) * rows] * expand(ysc_t, base, cols)
                    off = term if off is None else off + term
            y_heads = []
            for h in range(SSD_R):
                pf_col = e1s[cols, fwd + h:fwd + h + 1]
                rb_col = e1s[cols, bwd + h:bwd + h + 1]
                pf_row = e1s_t[fwd + h:fwd + h + 1, cols]
                rb_row = e1s_t[bwd + h:bwd + h + 1, cols]
                m_f = jnp.exp2(jnp.where(li <= si, pf_row - pf_col, NEG_INF)) * cb_t
                m_b = jnp.exp2(jnp.where(li >= si, rb_row - rb_col, NEG_INF)) * cb_t
                hs = slice(h * SSD_P, (h + 1) * SSD_P)
                lhs = jnp.concatenate([xdf[hs], xdb[hs]], axis=1).astype(BF16)
                rhs = jnp.concatenate([m_f, m_b], axis=0).astype(BF16)
                y_h = _dot(lhs, rhs)
                y_heads.append(y_h if off is None else y_h + off[hs])
            y_t = jnp.concatenate(y_heads, axis=0)
            y_ref[cols, y_cols] = y_t.T + x_ref[cols, y_cols] * d_ref[:, y_cols]


def _ssd_call(xbc, dt, dt_bias, a_log, d, state, fin_prev, layer, *, latent):
    seq = L_LAT if latent else L_CTX
    n_seq = N_LAT if latent else N_CTX
    gs = GS_LAT if latent else GS_CTX
    row0 = T_CTX // seq if latent else 0
    nc = seq // CHUNK
    xw, bw = gs * SSD_R * SSD_P, gs * SSD_N
    xcol0 = 0
    bcol0 = D_INNER // bw
    ccol0 = bcol0 + SSD_GROUPS // gs
    step_spec = lambda w: pl.BlockSpec((None, 1, w), lambda n, k: (k, 0, 0))
    in_specs = [
        pl.BlockSpec((seq, xw), lambda n, k: (row0 + n, xcol0 + k)),
        pl.BlockSpec((seq, bw), lambda n, k: (row0 + n, bcol0 + k)),
        pl.BlockSpec((seq, bw), lambda n, k: (row0 + n, ccol0 + k)),
        pl.BlockSpec((seq, 128), lambda n, k: (row0 + n, 0)),
        step_spec(128), step_spec(128), step_spec(xw),
    ]
    args = [xbc, xbc, xbc, dt, _step_lanes(dt_bias, gs), _step_lanes(-jnp.exp(a_log), gs),
            jnp.repeat(d, SSD_P).reshape(SSD_GROUPS // gs, 1, xw)]
    state_block = (None, None, 2, gs * SSD_R, SSD_P, SSD_N)
    aliases = {}
    if latent:
        in_specs.append(pl.BlockSpec(state_block, lambda n, k: (n, layer, 0, k, 0, 0)))
        args.append(state)
    elif fin_prev is not None:
        in_specs.append(pl.BlockSpec(memory_space=pl.ANY))
        args.append(fin_prev)
        aliases = {len(args) - 1: 1}
    y_shape = jax.ShapeDtypeStruct((n_seq * seq, D_INNER), F32)
    y_spec = pl.BlockSpec((seq, xw), lambda n, k: (n, k))
    if latent:
        out_shape, out_specs = y_shape, y_spec
    else:
        out_shape = (y_shape, jax.ShapeDtypeStruct((N_CTX, DEPTH // 2, 2, SSD_HEADS, SSD_P, SSD_N), F32))
        out_specs = (y_spec, pl.BlockSpec(state_block, lambda n, k: (n, layer, 0, k, 0, 0)))
    scratch = pltpu.VMEM((gs, nc, 2 * SSD_R * SSD_P, SSD_N), F32)
    return pl.pallas_call(
        functools.partial(_ssd_kernel, seq=seq, gs=gs, has_init=latent, has_final=not latent,
                          has_prev=bool(aliases)),
        out_shape=out_shape,
        grid=(n_seq, SSD_GROUPS // gs),
        in_specs=in_specs,
        out_specs=out_specs,
        input_output_aliases=aliases,
        scratch_shapes=[scratch, scratch],
        compiler_params=_cparams("arbitrary", "arbitrary"),
        name="ssd_scan_lat" if latent else "ssd_scan_ctx",
    )(*args)


def _step_lanes(p, gs):
    q = p.reshape(2, SSD_GROUPS, SSD_R).transpose(1, 0, 2).reshape(SSD_GROUPS // gs, 8 * gs)
    return jnp.pad(q, ((0, 0), (0, 128 - 8 * gs))).reshape(SSD_GROUPS // gs, 1, 128)


def kernel(x_prompt, x_sample, cache_k, cache_v, state_ssm, c, c_ctx, ada_w, ada_b, norm_mix_g, norm_ffn_g,
           ffn_w_gate_up, ffn_w_down, na_w_qkv, na_w_o, na_rpb, ssd_w_in, ssd_conv_w, ssd_conv_b,
           ssd_dt_bias, ssd_a_log, ssd_d, ssd_norm_g, ssd_w_out, final_norm_g):
    cond = jnp.zeros((N_COND, D), F32).at[0].set(c_ctx).at[1:1 + N_LAT].set(c)
    mods = _ada_call(cond, ada_w, ada_b).reshape(DEPTH, N_COND, 6, 1, D)
    x = (x_prompt.reshape(T_CTX, D), x_sample.reshape(T_LAT, D))
    final_g = final_norm_g.reshape(1, D)
    w_o = na_w_o.astype(BF16)
    w_in, w_out = ssd_w_in.astype(BF16), ssd_w_out.astype(BF16)
    w_gu, w_d = ffn_w_gate_up.astype(BF16), ffn_w_down.astype(BF16)
    conv_b = ssd_conv_b.reshape(DEPTH // 2, 1, CONV_DIM)
    cache_k_t, cache_v_t = cache_k.swapaxes(-1, -2), cache_v.swapaxes(-1, -2)

    new_kv, new_s = None, None
    for i in range(DEPTH):
        j = i // 2
        m = [mods[i, :, t] for t in range(6)]
        g_mix = norm_mix_g[i].reshape(1, D)
        if i % 2 == 0:
            qkv = _qkv_call(x, g_mix, m[0], m[1], na_w_qkv, j)
            o_ctx, *new_kv = _na_ctx_call(qkv, new_kv, j)
            t_even, t_odd = _bias_call(na_rpb[j])
            o_lat = _na_lat_call(qkv, cache_k_t, cache_v_t, j, t_even, t_odd)
            x = _oproj_call(o_ctx, o_lat, x, m[2], w_o, j)
        else:
            w_dt = jnp.pad(w_in[j, :, ZX_DIM:], ((0, 0), (0, 128 - 2 * SSD_HEADS)))
            z, xbc, dt = _inproj_call(x, g_mix, m[0], m[1], w_in, w_dt, ssd_conv_w, conv_b, j)
            scan_args = (xbc, dt, ssd_dt_bias[j], ssd_a_log[j], ssd_d[j])
            y_ctx, new_s = _ssd_call(*scan_args, None, new_s, j, latent=False)
            y_lat = _ssd_call(*scan_args, state_ssm, None, j, latent=True)
            x = _ssd_out_call(y_ctx, y_lat, z, x, m[2], ssd_norm_g[j].reshape(1, D_INNER), w_out, j)
        x = _ffn_call(x, norm_ffn_g[i].reshape(1, D), m[3], m[4], m[5], w_gu, w_d, final_g, i,
                      final_norm=(i == DEPTH - 1), split_out=(i == DEPTH - 1))

    y_prompt, y_sample = x
    return (y_prompt.reshape(N_CTX, L_CTX, D), y_sample.reshape(N_LAT, L_LAT, D), new_kv[0], new_kv[1], new_s)
```

```python
import functools

import jax
import jax.numpy as jnp
from jax import lax
from jax.experimental import pallas as pl
from jax.experimental.pallas import tpu as pltpu

F32 = jnp.float32
BF16 = jnp.bfloat16

D = 1024
N_CTX, L_CTX = 32, 256
N_LAT, L_LAT = 4, 1024
T_CTX = N_CTX * L_CTX
T_LAT = N_LAT * L_LAT
T_ALL = T_CTX + T_LAT
DEPTH = 4
N_COND = 8
PAST = 512
GRID_W = 64
GRID_H = L_LAT // GRID_W
HEADS = 16
HEAD_DIM = 64
HEAD_PAIRS = HEADS // 2
WIN_H, WIN_W = 8, 16
RPB_H, RPB_W = 2 * WIN_H - 1, 2 * WIN_W - 1
D_INNER = 2048
SSD_HEADS = 32
SSD_P = 64
SSD_GROUPS = 8
SSD_R = SSD_HEADS // SSD_GROUPS
SSD_N = 128
CHUNK = 128
CONV_DIM = D_INNER + 2 * SSD_GROUPS * SSD_N
ZX_DIM = D_INNER + CONV_DIM
D_FF = 2816
EPS = 1e-6
NEG_INF = float("-inf")
LOG2E = 1.4426950408889634

TM = 1024
TM_FFN = 1024
MXU_DIM = 256
FF_CUTS = (0, (D_FF // MXU_DIM + 1) // 2 * MXU_DIM, D_FF)
TN_QKV = 1024
TN_IN = 1024
MM_COLS = 512
TM_OUT = 512
GS_CTX = 8
GS_LAT = 4
BIAS_BLOCKS = 18
VMEM_LIMIT = 56 * 1024 * 1024


def _cparams(*sem):
    return pltpu.CompilerParams(dimension_semantics=sem, vmem_limit_bytes=VMEM_LIMIT)


def _cond_of_tile(i, tm):
    return jnp.maximum((i * tm) // L_LAT - (T_CTX // L_LAT - 1), 0)


def _silu(x):
    return x * (0.5 * jnp.tanh(0.5 * x) + 0.5)


def _rms(x, g):
    return x * lax.rsqrt(jnp.mean(x * x, axis=-1, keepdims=True) + EPS) * g


def _norm_mod(x, g, shift, scale):
    return _rms(x, g) * (1.0 + scale) + shift


def _dot(a, b):
    return jnp.dot(a, b, preferred_element_type=F32)


def _dot_nt(a, b):
    return lax.dot_general(a, b, (((1,), (1,)), ((), ())), preferred_element_type=F32)


def _split3(x):
    hi = x.astype(BF16)
    r = x - hi.astype(F32)
    mid = r.astype(BF16)
    lo = (r - mid.astype(F32)).astype(BF16)
    return hi, mid, lo


def _ada_kernel(c_ref, w_ref, b_ref, o_ref):
    s = _silu(c_ref[...]).astype(BF16)
    o_ref[...] = _dot(s, w_ref[...].astype(BF16)) + b_ref[...]


def _ada_call(cond, ada_w, ada_b):
    tn = 1536
    return pl.pallas_call(
        _ada_kernel,
        out_shape=jax.ShapeDtypeStruct((DEPTH, N_COND, 6 * D), F32),
        grid=(DEPTH, 6 * D // tn),
        in_specs=[
            pl.BlockSpec((N_COND, D), lambda l, j: (0, 0)),
            pl.BlockSpec((None, D, tn), lambda l, j: (l, 0, j)),
            pl.BlockSpec((None, 1, tn), lambda l, j: (l, 0, j)),
        ],
        out_specs=pl.BlockSpec((None, N_COND, tn), lambda l, j: (l, 0, j)),
        compiler_params=_cparams("arbitrary", "arbitrary"),
        name="ada",
    )(cond, ada_w, ada_b.reshape(DEPTH, 1, 6 * D))


def _mod_spec(tm):
    return pl.BlockSpec((None, 1, D), lambda i, *_: (_cond_of_tile(i, tm), 0, 0))


def _stream_specs(streams, tm, width):
    n_ctx = T_CTX // tm
    lat0 = 0 if isinstance(streams, tuple) else n_ctx
    arrays = streams if isinstance(streams, tuple) else (streams, streams)
    specs = [pl.BlockSpec((tm, width), lambda i, *_: (jnp.minimum(i, n_ctx - 1), 0)),
             pl.BlockSpec((tm, width), lambda i, *_: (lat0 + jnp.maximum(i - n_ctx, 0), 0))]
    return list(arrays), specs


def _on_stream(tm, fn, ctx_ref, lat_ref, extra=True):
    is_ctx = pl.program_id(0) < T_CTX // tm
    pl.when(is_ctx & extra)(lambda: fn(ctx_ref))
    pl.when(jnp.logical_not(is_ctx) & extra)(lambda: fn(lat_ref))


def _qkv_kernel(xc_ref, xl_ref, g_ref, sh_ref, sc_ref, w_ref, o_ref, h_scr, w_scr):
    @pl.when((pl.program_id(0) == 0) & (pl.program_id(1) == 0))
    def _():
        for t in range(w_scr.shape[0]):
            w_scr[t] = w_ref[:, t * TN_QKV:(t + 1) * TN_QKV].astype(BF16)

    def prologue(x_ref):
        h_scr[...] = _norm_mod(x_ref[...], g_ref[...], sh_ref[...], sc_ref[...]).astype(BF16)

    _on_stream(TM, prologue, xc_ref, xl_ref, pl.program_id(1) == 0)
    o_ref[...] = _dot(h_scr[...], w_scr[pl.program_id(1)])


def _qkv_call(x, g, shift, scale, w_all_f32, layer):
    n = w_all_f32.shape[2]
    x_args, x_specs = _stream_specs(x, TM, D)
    return pl.pallas_call(
        _qkv_kernel,
        out_shape=jax.ShapeDtypeStruct((T_ALL, n), F32),
        grid=(T_ALL // TM, n // TN_QKV),
        in_specs=x_specs + [
            pl.BlockSpec((1, D), lambda i, j: (0, 0)),
            _mod_spec(TM),
            _mod_spec(TM),
            pl.BlockSpec((None, D, n), lambda i, j: (layer, 0, 0), pipeline_mode=pl.Buffered(1)),
        ],
        out_specs=pl.BlockSpec((TM, TN_QKV), lambda i, j: (i, j)),
        scratch_shapes=[pltpu.VMEM((TM, D), BF16), pltpu.VMEM((n // TN_QKV, D, TN_QKV), BF16)],
        compiler_params=_cparams("arbitrary", "arbitrary"),
        name="na_qkv",
    )(*x_args, g, shift, scale, w_all_f32)


def _inproj_kernel(x_ref, g_ref, sh_ref, sc_ref, w_ref, wdt_ref, cw_ref, cb_ref, z_ref, xbc_ref, dt_ref,
                   h_scr, c_scr):
    i = pl.program_id(0)
    s = pl.program_id(1)
    n_z = D_INNER // TN_IN

    @pl.when((i == 0) & (s == 0))
    def _():
        c_scr[0:8, :] = jnp.zeros((8, TN_IN), F32)
        c_scr[TM + 8:TM + 16, :] = jnp.zeros((8, TN_IN), F32)

    @pl.when(s == 0)
    def _():
        h = _norm_mod(x_ref[...], g_ref[...], sh_ref[...], sc_ref[...]).astype(BF16)
        h_scr[...] = h
        dt_ref[...] = _dot(h, wdt_ref[...])

    @pl.when(s < n_z)
    def _():
        z_ref[...] = _dot(h_scr[...], w_ref[...])

    @pl.when(s >= n_z)
    def _():
        seq = jnp.where(i < T_CTX // TM, L_CTX, L_LAT)
        pos = lax.broadcasted_iota(jnp.int32, (TM, 1), 0) & (seq - 1)
        for m0 in range(0, TN_IN, MM_COLS):
            cols = slice(m0, m0 + MM_COLS)
            acc = _dot(h_scr[...], w_ref[:, cols])
            c_scr[8:TM + 8, cols] = acc
            prev = jnp.where(pos == 0, 0.0, c_scr[7:TM + 7, cols])
            nxt = jnp.where(pos == seq - 1, 0.0, c_scr[9:TM + 9, cols])
            y = cw_ref[0:1, cols] * prev + cw_ref[1:2, cols] * acc + cw_ref[2:3, cols] * nxt + cb_ref[:, cols]
            xbc_ref[:, cols] = _silu(y)


def _inproj_call(x, g, shift, scale, w_all, w_dt, conv_w_all, conv_b_all, layer):
    n_z = D_INNER // TN_IN
    n_steps = ZX_DIM // TN_IN
    conv_tile = lambda s: jnp.maximum(s - n_z, 0)
    return pl.pallas_call(
        _inproj_kernel,
        out_shape=(jax.ShapeDtypeStruct((T_ALL, D_INNER), F32),
                   jax.ShapeDtypeStruct((T_ALL, CONV_DIM), F32),
                   jax.ShapeDtypeStruct((T_ALL, 128), F32)),
        grid=(T_ALL // TM, n_steps),
        in_specs=[
            pl.BlockSpec((TM, D), lambda i, s: (i, 0)),
            pl.BlockSpec((1, D), lambda i, s: (0, 0)),
            _mod_spec(TM),
            _mod_spec(TM),
            pl.BlockSpec((None, D, TN_IN), lambda i, s: (layer, 0, s)),
            pl.BlockSpec((D, 128), lambda i, s: (0, 0)),
            pl.BlockSpec((None, 3, TN_IN), lambda i, s: (layer, 0, conv_tile(s))),
            pl.BlockSpec((None, 1, TN_IN), lambda i, s: (layer, 0, conv_tile(s))),
        ],
        out_specs=(pl.BlockSpec((TM, TN_IN), lambda i, s: (i, jnp.minimum(s, n_z - 1))),
                   pl.BlockSpec((TM, TN_IN), lambda i, s: (i, conv_tile(s))),
                   pl.BlockSpec((TM, 128), lambda i, s: (i, 0))),
        scratch_shapes=[pltpu.VMEM((TM, D), BF16), pltpu.VMEM((TM + 16, TN_IN), F32)],
        compiler_params=_cparams("arbitrary", "arbitrary"),
        name="ssd_inproj",
    )(x, g, shift, scale, w_all, w_dt, conv_w_all, conv_b_all)


def _round_weight_once(w_ref, w_scr):
    @pl.when(pl.program_id(0) == 0)
    def _():
        w_scr[...] = w_ref[...].astype(BF16)


def _oproj_kernel(ac_ref, al_ref, xc_ref, xl_ref, gt_ref, w_ref, o_ref, w_scr):
    _round_weight_once(w_ref, w_scr)

    def run(refs):
        a_ref, x_ref = refs
        o_ref[...] = x_ref[...] + gt_ref[...] * _dot(a_ref[...], w_scr[...])

    _on_stream(TM_OUT, run, (ac_ref, xc_ref), (al_ref, xl_ref))


def _oproj_call(a_ctx, a_lat, x, gate, w_all, layer):
    a_args, a_specs = _stream_specs((a_ctx, a_lat), TM_OUT, D)
    x_args, x_specs = _stream_specs(x, TM_OUT, D)
    return pl.pallas_call(
        _oproj_kernel,
        out_shape=jax.ShapeDtypeStruct((T_ALL, D), F32),
        grid=(T_ALL // TM_OUT,),
        in_specs=a_specs + x_specs + [
            _mod_spec(TM_OUT),
            pl.BlockSpec((None, D, D), lambda i: (layer, 0, 0), pipeline_mode=pl.Buffered(1)),
        ],
        out_specs=pl.BlockSpec((TM_OUT, D), lambda i: (i, 0)),
        scratch_shapes=[pltpu.VMEM((D, D), BF16)],
        compiler_params=_cparams("arbitrary"),
        name="na_oproj",
    )(*a_args, *x_args, gate, w_all)


def _ssd_out_kernel(yc_ref, yl_ref, z_ref, x_ref, gt_ref, ng_ref, w_ref, o_ref, w_scr):
    _round_weight_once(w_ref, w_scr)

    def run(y_ref):
        v = y_ref[...] * _silu(z_ref[...])
        v = _rms(v, ng_ref[...]).astype(BF16)
        o_ref[...] = x_ref[...] + gt_ref[...] * _dot(v, w_scr[...])

    _on_stream(TM_OUT, run, yc_ref, yl_ref)


def _ssd_out_call(y_ctx, y_lat, z, x, gate, norm_g, w_all, layer):
    y_args, y_specs = _stream_specs((y_ctx, y_lat), TM_OUT, D_INNER)
    return pl.pallas_call(
        _ssd_out_kernel,
        out_shape=jax.ShapeDtypeStruct((T_ALL, D), F32),
        grid=(T_ALL // TM_OUT,),
        in_specs=y_specs + [
            pl.BlockSpec((TM_OUT, D_INNER), lambda i: (i, 0)),
            pl.BlockSpec((TM_OUT, D), lambda i: (i, 0)),
            _mod_spec(TM_OUT),
            pl.BlockSpec((1, D_INNER), lambda i: (0, 0)),
            pl.BlockSpec((None, D_INNER, D), lambda i: (layer, 0, 0), pipeline_mode=pl.Buffered(1)),
        ],
        out_specs=pl.BlockSpec((TM_OUT, D), lambda i: (i, 0)),
        scratch_shapes=[pltpu.VMEM((D_INNER, D), BF16)],
        compiler_params=_cparams("arbitrary"),
        name="ssd_out",
    )(*y_args, z, x, gate, norm_g, w_all)


def _ffn_kernel(x_ref, g_ref, sh_ref, sc_ref, gt_ref, wgu_ref, wd_ref, fg_ref, *o_refs, final_norm):
    half = TM_FFN // 2
    for r in range(2):
        rows = slice(r * half, (r + 1) * half)
        h = _norm_mod(x_ref[rows, :], g_ref[...], sh_ref[...], sc_ref[...]).astype(BF16)
        acc = None
        for lo, hi in zip(FF_CUTS[:-1], FF_CUTS[1:]):
            gate_cols = slice(lo, hi)
            up_cols = slice(D_FF + lo, D_FF + hi)
            act = (_silu(_dot(h, wgu_ref[:, gate_cols])) * _dot(h, wgu_ref[:, up_cols])).astype(BF16)
            part = _dot(act, wd_ref[gate_cols, :])
            acc = part if acc is None else acc + part
        y = x_ref[rows, :] + gt_ref[...] * acc
        if final_norm:
            y = _rms(y, fg_ref[...])
        if len(o_refs) == 1:
            o_refs[0][rows, :] = y
        else:
            def store(o_ref, y=y, rows=rows):
                o_ref[rows, :] = y

            _on_stream(TM_FFN, store, *o_refs)


def _ffn_call(x, g, shift, scale, gate, w_gu_all, w_d_all, final_g, layer, final_norm, split_out=False):
    resident = dict(pipeline_mode=pl.Buffered(1))
    n_ctx = T_CTX // TM_FFN
    if split_out:
        out_shape = (jax.ShapeDtypeStruct((T_CTX, D), F32), jax.ShapeDtypeStruct((T_LAT, D), F32))
        out_specs = (pl.BlockSpec((TM_FFN, D), lambda i: (jnp.minimum(i, n_ctx - 1), 0)),
                     pl.BlockSpec((TM_FFN, D), lambda i: (jnp.maximum(i - n_ctx, 0), 0)))
    else:
        out_shape = jax.ShapeDtypeStruct((T_ALL, D), F32)
        out_specs = pl.BlockSpec((TM_FFN, D), lambda i: (i, 0))
    return pl.pallas_call(
        functools.partial(_ffn_kernel, final_norm=final_norm),
        out_shape=out_shape,
        grid=(T_ALL // TM_FFN,),
        in_specs=[
            pl.BlockSpec((TM_FFN, D), lambda i: (i, 0)),
            pl.BlockSpec((1, D), lambda i: (0, 0)),
            _mod_spec(TM_FFN),
            _mod_spec(TM_FFN),
            _mod_spec(TM_FFN),
            pl.BlockSpec((None, D, 2 * D_FF), lambda i: (layer, 0, 0), **resident),
            pl.BlockSpec((None, D_FF, D), lambda i: (layer, 0, 0), **resident),
            pl.BlockSpec((1, D), lambda i: (0, 0)),
        ],
        out_specs=out_specs,
        compiler_params=_cparams("arbitrary"),
        name="ffn",
    )(x, g, shift, scale, gate, w_gu_all, w_d_all, final_g)


def _bias_kernel(rpb_ref, te_ref, to_ref):
    qc = lax.broadcasted_iota(jnp.int32, (GRID_W, 128), 0)
    lane = lax.broadcasted_iota(jnp.int32, (GRID_W, 128), 1)
    kc = lane & (GRID_W - 1)
    upper = lane >= GRID_W
    start = jnp.clip(qc - WIN_W // 2, 0, GRID_W - WIN_W)
    in_win = (kc >= start) & (kc < start + WIN_W)

    def block(dr, lane0):
        base = jnp.broadcast_to(rpb_ref[dr:dr + 1, :], (GRID_W, 128))
        return pltpu.roll(base, (lane0 - (WIN_W - 1)) % 128, 1, stride=1, stride_axis=0)

    def pair_tile(dr_lo, dr_hi):
        def ok(dr):
            return 0 <= dr < RPB_H

        if ok(dr_lo) and ok(dr_hi):
            return jnp.where(in_win, jnp.where(upper, block(dr_hi, GRID_W), block(dr_lo, 0)), NEG_INF)
        if ok(dr_lo):
            return jnp.where(in_win & jnp.logical_not(upper), block(dr_lo, 0), NEG_INF)
        if ok(dr_hi):
            return jnp.where(in_win & upper, block(dr_hi, GRID_W), NEG_INF)
        return jnp.full((GRID_W, 128), NEG_INF, F32)

    for t in range(BIAS_BLOCKS // 2):
        te_ref[:, t * 128:(t + 1) * 128] = pair_tile(2 * t - 1, 2 * t)
        to_ref[:, t * 128:(t + 1) * 128] = pair_tile(2 * t - 2, 2 * t - 1)


def _bias_call(rpb):
    shape = jax.ShapeDtypeStruct((HEADS, GRID_W, BIAS_BLOCKS * GRID_W), F32)
    spec = pl.BlockSpec((None, GRID_W, BIAS_BLOCKS * GRID_W), lambda h: (h, 0, 0))
    rows = jnp.pad(rpb, ((0, 0), (0, 16 - RPB_H), (0, 128 - RPB_W)))
    return pl.pallas_call(
        _bias_kernel,
        out_shape=(shape, shape),
        grid=(HEADS,),
        in_specs=[pl.BlockSpec((None, 16, 128), lambda h: (h, 0, 0))],
        out_specs=(spec, spec),
        compiler_params=_cparams("arbitrary"),
        name="na_bias",
    )(rows)


def _half_mask(e):
    lane = lax.broadcasted_iota(jnp.int32, (1, 128), 1)
    return (lane >= HEAD_DIM) if e else (lane < HEAD_DIM)


def _na_ctx_kernel(qkv_ref, *refs):
    o_ref, ko_ref, vo_ref = refs[-3:]
    scale = HEAD_DIM ** -0.5
    for hp in range(HEAD_PAIRS):
        q2 = qkv_ref[:, hp * 128:(hp + 1) * 128] * scale
        k2 = qkv_ref[:, D + hp * 128:D + (hp + 1) * 128]
        v2 = qkv_ref[:, 2 * D + hp * 128:2 * D + (hp + 1) * 128]
        for e in range(2):
            ko_ref[2 * hp + e] = k2[:, e * HEAD_DIM:(e + 1) * HEAD_DIM]
            vo_ref[2 * hp + e] = v2[:, e * HEAD_DIM:(e + 1) * HEAD_DIM]
        kb = k2.astype(BF16)
        vb = v2.astype(BF16)
        outs = []
        for e in range(2):
            qm = jnp.where(_half_mask(e), q2, 0.0).astype(BF16)
            s = _dot_nt(qm, kb)
            p = jnp.exp(s - jnp.max(s, axis=-1, keepdims=True))
            l = jnp.sum(p, axis=-1, keepdims=True)
            outs.append(_dot(p.astype(BF16), vb) / l)
        o_ref[:, hp * 128:(hp + 1) * 128] = jnp.where(_half_mask(0), outs[0], outs[1]).astype(BF16)


def _na_ctx_call(qkv, kv_prev, layer):
    kv_shape = jax.ShapeDtypeStruct((N_CTX, (DEPTH + 1) // 2, HEADS, L_CTX, HEAD_DIM), F32)
    kv_spec = pl.BlockSpec((None, None, HEADS, L_CTX, HEAD_DIM), lambda n: (n, layer, 0, 0, 0))
    in_specs = [pl.BlockSpec((L_CTX, 3 * D), lambda n: (n, 0))]
    args = [qkv]
    aliases = {}
    if kv_prev is not None:
        in_specs += [pl.BlockSpec(memory_space=pl.ANY)] * 2
        args += list(kv_prev)
        aliases = {1: 1, 2: 2}
    return pl.pallas_call(
        _na_ctx_kernel,
        out_shape=(jax.ShapeDtypeStruct((T_CTX, D), BF16), kv_shape, kv_shape),
        grid=(N_CTX,),
        in_specs=in_specs,
        out_specs=(pl.BlockSpec((L_CTX, D), lambda n: (n, 0)), kv_spec, kv_spec),
        input_output_aliases=aliases,
        compiler_params=_cparams("arbitrary"),
        name="na_ctx",
    )(*args)


def _window_of_row(r):
    r0 = min(max(r - WIN_H // 2, 0), GRID_H - WIN_H)
    r0a = r0 & ~1
    rows = WIN_H if r0a == r0 else WIN_H + 2
    return r0, r0a, rows


def _na_bands():
    bands = []
    for rs in (range(0, GRID_H // 2), range(GRID_H // 2, GRID_H)):
        lo = min(_window_of_row(r)[1] for r in rs)
        hi = max(_window_of_row(r)[1] + _window_of_row(r)[2] for r in rs)
        bands.append((slice(rs[0] * GRID_W, (rs[-1] + 1) * GRID_W), slice(lo * GRID_W, hi * GRID_W)))
    return tuple(bands)


_NA_BANDS = _na_bands()


def _na_lat_kernel(q_ref, k_ref, v_ref, kc_ref, vc_ref, te_ref, to_ref, o_ref, *scratch):
    @pl.when((pl.program_id(0) == 0) & (pl.program_id(1) == 0))
    def _():
        scratch[2][...] = jnp.zeros_like(scratch[2])

    q2 = q_ref[...] * (HEAD_DIM ** -0.5)
    kb = k_ref[...].astype(BF16)
    vb = v_ref[...].astype(BF16)
    kc_t = jnp.concatenate([kc_ref[0], kc_ref[1]], axis=0).astype(BF16)
    vc_t = jnp.concatenate([vc_ref[0], vc_ref[1]], axis=0).astype(BF16)
    outs = []
    for e in range(2):
        s_scr, sc_scr, p_scr, pc_scr, li_scr = (ref.at[e] for ref in scratch)
        qm = jnp.where(_half_mask(e), q2, 0.0).astype(BF16)
        for q_rows, k_rows in _NA_BANDS:
            s_scr[q_rows, k_rows] = _dot_nt(qm[q_rows], kb[k_rows])
        sc_scr[...] = _dot(qm, kc_t)
        for r in range(GRID_H):
            r0, r0a, nrows = _window_of_row(r)
            width = nrows * GRID_W
            blk = r0a - r + WIN_H
            t_ref, t_off = (te_ref, blk * GRID_W) if r % 2 == 0 else (to_ref, (blk + 1) * GRID_W)
            rows = slice(r * GRID_W, (r + 1) * GRID_W)
            cols = slice(r0a * GRID_W, r0a * GRID_W + width)
            sl = s_scr[rows, cols] + t_ref[e, :, t_off:t_off + width]
            if nrows != WIN_H:
                lane = lax.broadcasted_iota(jnp.int32, (1, width), 1)
                sl = jnp.where((lane < GRID_W) | (lane >= width - GRID_W), NEG_INF, sl)
            sc = sc_scr[rows, :]
            m = jnp.maximum(jnp.max(sl, axis=-1, keepdims=True), jnp.max(sc, axis=-1, keepdims=True))
            p_loc = jnp.exp(sl - m)
            p_ctx = jnp.exp(sc - m)
            l = jnp.sum(p_loc, axis=-1, keepdims=True) + jnp.sum(p_ctx, axis=-1, keepdims=True)
            p_scr[rows, cols] = p_loc.astype(BF16)
            pc_scr[rows, :] = p_ctx.astype(BF16)
            li_scr[rows, :] = jnp.broadcast_to(1.0 / l, (GRID_W, 128))
        o_loc = jnp.concatenate([_dot(p_scr[q_rows, k_rows], vb[k_rows]) for q_rows, k_rows in _NA_BANDS], axis=0)
        outs.append((o_loc + _dot_nt(pc_scr[...], vc_t)) * li_scr[...])
    o_ref[...] = jnp.where(_half_mask(0), outs[0], outs[1]).astype(BF16)


def _na_lat_call(qkv, cache_k, cache_v, layer, t_even, t_odd):
    row0 = T_CTX // L_LAT
    ctx_spec = pl.BlockSpec((None, None, 2, HEAD_DIM, PAST), lambda b, hp: (b, layer, hp, 0, 0))
    t_spec = pl.BlockSpec((2, GRID_W, BIAS_BLOCKS * GRID_W), lambda b, hp: (hp, 0, 0))
    return pl.pallas_call(
        _na_lat_kernel,
        out_shape=jax.ShapeDtypeStruct((T_LAT, D), BF16),
        grid=(N_LAT, HEAD_PAIRS),
        in_specs=[
            pl.BlockSpec((L_LAT, 128), lambda b, hp: (row0 + b, hp)),
            pl.BlockSpec((L_LAT, 128), lambda b, hp: (row0 + b, HEAD_PAIRS + hp)),
            pl.BlockSpec((L_LAT, 128), lambda b, hp: (row0 + b, 2 * HEAD_PAIRS + hp)),
            ctx_spec, ctx_spec, t_spec, t_spec,
        ],
        out_specs=pl.BlockSpec((L_LAT, 128), lambda b, hp: (b, hp)),
        scratch_shapes=[
            pltpu.VMEM((2, L_LAT, L_LAT), F32),
            pltpu.VMEM((2, L_LAT, PAST), F32),
            pltpu.VMEM((2, L_LAT, L_LAT), BF16),
            pltpu.VMEM((2, L_LAT, PAST), BF16),
            pltpu.VMEM((2, L_LAT, 128), F32),
        ],
        compiler_params=_cparams("arbitrary", "arbitrary"),
        name="na_lat",
    )(qkv, qkv, qkv, cache_k, cache_v, t_even, t_odd)


def _ssd_kernel(*refs, seq, gs, has_init, has_final, has_prev):
    it = iter(refs)
    x_ref, b_ref, c_ref, dt_ref, bias_ref, nega_ref, d_ref = (next(it) for _ in range(7))
    init_ref = next(it) if has_init else None
    if has_prev:
        next(it)
    y_ref = next(it)
    fin_ref = next(it) if has_final else None
    sloc_scr, sin_scr = next(it), next(it)

    nc = seq // CHUNK
    g0 = pl.program_id(1) * gs
    rows = SSD_R * SSD_P

    src = lax.broadcasted_iota(jnp.int32, (128, 128), 0)
    dst = lax.broadcasted_iota(jnp.int32, (128, 128), 1)
    want = ((dst >> 2) & 1) * SSD_HEADS + (g0 + (dst >> 3)) * SSD_R + (dst & 3)
    sel = jnp.where((dst < 8 * gs) & (src == want), 1.0, 0.0).astype(BF16)
    dt_raw = sum(_dot(p, sel) for p in _split3(dt_ref[...]))
    u = dt_raw + bias_ref[...]
    dt = jnp.maximum(u, 0.0) + jnp.log1p(jnp.exp(-jnp.abs(u)))
    a = dt * nega_ref[...]

    li = lax.broadcasted_iota(jnp.int32, (CHUNK, CHUNK), 0)
    si = lax.broadcasted_iota(jnp.int32, (CHUNK, CHUNK), 1)
    tril = jnp.where(si <= li, 1.0, 0.0).astype(BF16)
    triu = jnp.where(si >= li, 1.0, 0.0).astype(BF16)
    pre, suf = [], []
    for c in range(nc):
        parts = _split3(a[c * CHUNK:(c + 1) * CHUNK])
        pre.append(sum(_dot(tril, p) for p in parts))
        suf.append(sum(_dot(triu, p) for p in parts))
    pre = jnp.concatenate(pre, axis=0)
    suf = jnp.concatenate(suf, axis=0)
    fwd_lane = (lax.broadcasted_iota(jnp.int32, (1, 128), 1) & SSD_R) == 0
    e1 = jnp.where(fwd_lane, pre, suf)
    e2 = jnp.where(fwd_lane, suf, pre) - a
    e1s = e1 * LOG2E
    e1s_t = e1s.T
    dt_t = dt.T
    ysc_t = jnp.exp(e1).T
    w2_t = jnp.exp(e2).T

    def expand(t, off, cols):
        return jnp.concatenate(
            [jnp.broadcast_to(t[off + h:off + h + 1, cols], (SSD_P, cols.stop - cols.start))
             for h in range(SSD_R)], axis=0)

    def chunk_decay(off, col):
        return expand(ysc_t, off, slice(col, col + 1))

    x_t = x_ref[...].T

    for gi in range(gs):
        fwd, bwd = 8 * gi, 8 * gi + SSD_R
        xg_t = x_t[gi * rows:(gi + 1) * rows]
        b_cols = slice(gi * SSD_N, (gi + 1) * SSD_N)
        y_cols = slice(gi * rows, (gi + 1) * rows)

        for c in range(nc):
            cols = slice(c * CHUNK, (c + 1) * CHUNK)
            xc = xg_t[:, cols]
            xw = jnp.concatenate([xc * (expand(dt_t, fwd, cols) * expand(w2_t, fwd, cols)),
                                  xc * (expand(dt_t, bwd, cols) * expand(w2_t, bwd, cols))], axis=0)
            sloc_scr[gi, c] = _dot(xw.astype(BF16), b_ref[cols, b_cols].astype(BF16))

        heads = slice(gi * SSD_R, (gi + 1) * SSD_R)
        s_f = init_ref[0, heads].reshape(rows, SSD_N) if has_init else None
        for c in range(nc):
            if s_f is not None:
                sin_scr[gi, c, 0:rows] = s_f
                s_f = s_f * chunk_decay(fwd, c * CHUNK + CHUNK - 1) + sloc_scr[gi, c, 0:rows]
            else:
                s_f = sloc_scr[gi, c, 0:rows]
        s_b = init_ref[1, heads].reshape(rows, SSD_N) if has_init else None
        for c in reversed(range(nc)):
            if s_b is not None:
                sin_scr[gi, c, rows:2 * rows] = s_b
                s_b = s_b * chunk_decay(bwd, c * CHUNK) + sloc_scr[gi, c, rows:2 * rows]
            else:
                s_b = sloc_scr[gi, c, rows:2 * rows]
        if has_final:
            fin_ref[0, heads] = s_f.reshape(SSD_R, SSD_P, SSD_N)
            fin_ref[1, heads] = s_b.reshape(SSD_R, SSD_P, SSD_N)

        for c in range(nc):
            cols = slice(c * CHUNK, (c + 1) * CHUNK)
            bc = b_ref[cols, b_cols].astype(BF16)
            cc = c_ref[cols, b_cols].astype(BF16)
            cb_t = _dot_nt(bc, cc)
            xc = xg_t[:, cols]
            xdf = xc * expand(dt_t, fwd, cols)
            xdb = xc * expand(dt_t, bwd, cols)
            entering = []
            if has_init or c > 0:
                entering.append((slice(0, rows), fwd))
            if has_init or c < nc - 1:
                entering.append((slice(rows, 2 * rows), bwd))
            off = None
            if entering:
                s_in = jnp.concatenate([sin_scr[gi, c, r] for r, _ in entering], axis=0)
                prod = _dot_nt(s_in.astype(BF16), cc)
                for k, (_, base) in enumerate(entering):
                    term = prod[k * rows:(k + 1) * rows] * expand(ysc_t, base, cols)
                    off = term if off is None else off + term
            y_heads = []
            for h in range(SSD_R):
                pf_col = e1s[cols, fwd + h:fwd + h + 1]
                rb_col = e1s[cols, bwd + h:bwd + h + 1]
                pf_row = e1s_t[fwd + h:fwd + h + 1, cols]
                rb_row = e1s_t[bwd + h:bwd + h + 1, cols]
                m_f = jnp.exp2(jnp.where(li <= si, pf_row - pf_col, NEG_INF)) * cb_t
                m_b = jnp.exp2(jnp.where(li >= si, rb_row - rb_col, NEG_INF)) * cb_t
                hs = slice(h * SSD_P, (h + 1) * SSD_P)
                lhs = jnp.concatenate([xdf[hs], xdb[hs]], axis=1).astype(BF16)
                rhs = jnp.concatenate([m_f, m_b], axis=0).astype(BF16)
                y_h = _dot(lhs, rhs)
                y_heads.append(y_h if off is None else y_h + off[hs])
            y_t = jnp.concatenate(y_heads, axis=0)
            y_ref[cols, y_cols] = y_t.T + x_ref[cols, y_cols] * d_ref[:, y_cols]


def _ssd_call(xbc, dt, dt_bias, a_log, d, state, fin_prev, layer, *, latent):
    seq = L_LAT if latent else L_CTX
    n_seq = N_LAT if latent else N_CTX
    gs = GS_LAT if latent else GS_CTX
    row0 = T_CTX // seq if latent else 0
    nc = seq // CHUNK
    xw, bw = gs * SSD_R * SSD_P, gs * SSD_N
    xcol0 = 0
    bcol0 = D_INNER // bw
    ccol0 = bcol0 + SSD_GROUPS // gs
    step_spec = lambda w: pl.BlockSpec((None, 1, w), lambda n, k: (k, 0, 0))
    in_specs = [
        pl.BlockSpec((seq, xw), lambda n, k: (row0 + n, xcol0 + k)),
        pl.BlockSpec((seq, bw), lambda n, k: (row0 + n, bcol0 + k)),
        pl.BlockSpec((seq, bw), lambda n, k: (row0 + n, ccol0 + k)),
        pl.BlockSpec((seq, 128), lambda n, k: (row0 + n, 0)),
        step_spec(128), step_spec(128), step_spec(xw),
    ]
    args = [xbc, xbc, xbc, dt, _step_lanes(dt_bias, gs), _step_lanes(-jnp.exp(a_log), gs),
            jnp.repeat(d, SSD_P).reshape(SSD_GROUPS // gs, 1, xw)]
    state_block = (None, None, 2, gs * SSD_R, SSD_P, SSD_N)
    aliases = {}
    if latent:
        in_specs.append(pl.BlockSpec(state_block, lambda n, k: (n, layer, 0, k, 0, 0)))
        args.append(state)
    elif fin_prev is not None:
        in_specs.append(pl.BlockSpec(memory_space=pl.ANY))
        args.append(fin_prev)
        aliases = {len(args) - 1: 1}
    y_shape = jax.ShapeDtypeStruct((n_seq * seq, D_INNER), F32)
    y_spec = pl.BlockSpec((seq, xw), lambda n, k: (n, k))
    if latent:
        out_shape, out_specs = y_shape, y_spec
    else:
        out_shape = (y_shape, jax.ShapeDtypeStruct((N_CTX, DEPTH // 2, 2, SSD_HEADS, SSD_P, SSD_N), F32))
        out_specs = (y_spec, pl.BlockSpec(state_block, lambda n, k: (n, layer, 0, k, 0, 0)))
    scratch = pltpu.VMEM((gs, nc, 2 * SSD_R * SSD_P, SSD_N), F32)
    return pl.pallas_call(
        functools.partial(_ssd_kernel, seq=seq, gs=gs, has_init=latent, has_final=not latent,
                          has_prev=bool(aliases)),
        out_shape=out_shape,
        grid=(n_seq, SSD_GROUPS // gs),
        in_specs=in_specs,
        out_specs=out_specs,
        input_output_aliases=aliases,
        scratch_shapes=[scratch, scratch],
        compiler_params=_cparams("arbitrary", "arbitrary"),
        name="ssd_scan_lat" if latent else "ssd_scan_ctx",
    )(*args)


def _step_lanes(p, gs):
    q = p.reshape(2, SSD_GROUPS, SSD_R).transpose(1, 0, 2).reshape(SSD_GROUPS // gs, 8 * gs)
    return jnp.pad(q, ((0, 0), (0, 128 - 8 * gs))).reshape(SSD_GROUPS // gs, 1, 128)


def kernel(x_prompt, x_sample, cache_k, cache_v, state_ssm, c, c_ctx, ada_w, ada_b, norm_mix_g, norm_ffn_g,
           ffn_w_gate_up, ffn_w_down, na_w_qkv, na_w_o, na_rpb, ssd_w_in, ssd_conv_w, ssd_conv_b,
           ssd_dt_bias, ssd_a_log, ssd_d, ssd_norm_g, ssd_w_out, final_norm_g):
    cond = jnp.zeros((N_COND, D), F32).at[0].set(c_ctx).at[1:1 + N_LAT].set(c)
    mods = _ada_call(cond, ada_w, ada_b).reshape(DEPTH, N_COND, 6, 1, D)
    x = (x_prompt.reshape(T_CTX, D), x_sample.reshape(T_LAT, D))
    final_g = final_norm_g.reshape(1, D)
    w_o, w_out = na_w_o, ssd_w_out
    w_in = ssd_w_in.astype(BF16)
    w_gu, w_d = ffn_w_gate_up.astype(BF16), ffn_w_down.astype(BF16)
    conv_b = ssd_conv_b.reshape(DEPTH // 2, 1, CONV_DIM)
    cache_k_t, cache_v_t = cache_k.swapaxes(-1, -2), cache_v.swapaxes(-1, -2)

    new_kv, new_s = None, None
    for i in range(DEPTH):
        j = i // 2
        m = [mods[i, :, t] for t in range(6)]
        g_mix = norm_mix_g[i].reshape(1, D)
        if i % 2 == 0:
            qkv = _qkv_call(x, g_mix, m[0], m[1], na_w_qkv, j)
            o_ctx, *new_kv = _na_ctx_call(qkv, new_kv, j)
            t_even, t_odd = _bias_call(na_rpb[j])
            o_lat = _na_lat_call(qkv, cache_k_t, cache_v_t, j, t_even, t_odd)
            x = _oproj_call(o_ctx, o_lat, x, m[2], w_o, j)
        else:
            w_dt = jnp.pad(w_in[j, :, ZX_DIM:], ((0, 0), (0, 128 - 2 * SSD_HEADS)))
            z, xbc, dt = _inproj_call(x, g_mix, m[0], m[1], w_in, w_dt, ssd_conv_w, conv_b, j)
            scan_args = (xbc, dt, ssd_dt_bias[j], ssd_a_log[j], ssd_d[j])
            y_ctx, new_s = _ssd_call(*scan_args, None, new_s, j, latent=False)
            y_lat = _ssd_call(*scan_args, state_ssm, None, j, latent=True)
            x = _ssd_out_call(y_ctx, y_lat, z, x, m[2], ssd_norm_g[j].reshape(1, D_INNER), w_out, j)
        x = _ffn_call(x, norm_ffn_g[i].reshape(1, D), m[3], m[4], m[5], w_gu, w_d, final_g, i,
                      final_norm=(i == DEPTH - 1), split_out=(i == DEPTH - 1))

    y_prompt, y_sample = x
    return (y_prompt.reshape(N_CTX, L_CTX, D), y_sample.reshape(N_LAT, L_LAT, D), new_kv[0], new_kv[1], new_s)
```
